```python
import math
import jax, jax.numpy as jnp
from jax import lax
import numpy as np

D_MODEL = 1024
BATCH = 8
SEQ = 4096
DEPTH = 1

HEAD_DIM = 64
ATTN_WIDTH = D_MODEL // 2
CONV_WIDTH = D_MODEL - ATTN_WIDTH
N_Q_HEADS = ATTN_WIDTH // HEAD_DIM
N_KV_HEADS = max(1, N_Q_HEADS // 4)
Q_PER_KV = N_Q_HEADS // N_KV_HEADS
KV_WIDTH = N_KV_HEADS * HEAD_DIM
IN_COLS = ATTN_WIDTH + 2 * KV_WIDTH + 2 * CONV_WIDTH
WINDOW = 128
ATTN_BLOCK = 128
CONV_KERNEL = 31

N_EXPERTS = 256
TOP_K = 8
N_EXPERT_GROUPS = 8
TOPK_EXPERT_GROUPS = 4
D_EXPERT = D_MODEL // 4
D_SHARED = D_EXPERT
ROUTED_SCALE = 2.5
MOE_BLOCK = 128

LN_EPS = 1e-5
DEEPNORM_ALPHA = (2.0 * DEPTH) ** 0.25
DEEPNORM_BETA = (8.0 * DEPTH) ** -0.25

kernel_name = "hybrid_swa_conformer_moe_deepnorm_adaln"


def layer_norm(x, g, b):
    xf = x.astype(jnp.float32)
    mu = jnp.mean(xf, axis=-1, keepdims=True)
    var = jnp.mean(jnp.square(xf - mu), axis=-1, keepdims=True)
    y = (xf - mu) * lax.rsqrt(var + LN_EPS) * g.astype(jnp.float32) + b.astype(jnp.float32)
    return y.astype(x.dtype)


def alibi_slopes(n_heads):
    return jnp.asarray([2.0 ** (-8.0 * (i + 1) / n_heads) for i in range(n_heads)], dtype=jnp.float32)


def sliding_window_attention(q, k, v, sinks):
    B, S = q.shape[0], q.shape[1]
    nb = S // ATTN_BLOCK
    qb = q.reshape(B, nb, ATTN_BLOCK, N_KV_HEADS, Q_PER_KV, HEAD_DIM)

    def band(t):
        tp = jnp.pad(t, ((0, 0), (ATTN_BLOCK, 0), (0, 0), (0, 0)))[:, :S]
        prev = tp.reshape(B, nb, ATTN_BLOCK, N_KV_HEADS, HEAD_DIM)
        cur = t.reshape(B, nb, ATTN_BLOCK, N_KV_HEADS, HEAD_DIM)
        return jnp.concatenate([prev, cur], axis=2)

    kb, vb = band(k), band(v)
    scale = 1.0 / math.sqrt(HEAD_DIM)
    scores = jnp.einsum('bnqkgd,bnjkd->bkgnqj', qb, kb).astype(jnp.float32) * scale

    qi = jnp.arange(ATTN_BLOCK)[:, None]
    kj = jnp.arange(2 * ATTN_BLOCK)[None, :]
    dist = ATTN_BLOCK + qi - kj
    key_pos = (jnp.arange(nb)[:, None, None] - 1) * ATTN_BLOCK + kj[None]
    valid = (dist[None] >= 0) & (dist[None] < WINDOW) & (key_pos >= 0)

    slopes = alibi_slopes(N_Q_HEADS).reshape(N_KV_HEADS, Q_PER_KV, 1, 1, 1)
    scores = scores - slopes * dist.astype(jnp.float32)
    scores = jnp.where(valid, scores, -jnp.inf)

    sink = sinks.astype(jnp.float32).reshape(N_KV_HEADS, Q_PER_KV, 1, 1, 1)
    m = jnp.maximum(jnp.max(scores, axis=-1, keepdims=True), sink)
    p = jnp.exp(scores - m)
    denom = jnp.sum(p, axis=-1, keepdims=True) + jnp.exp(sink - m)
    probs = (p / denom).astype(v.dtype)
    out = jnp.einsum('bkgnqj,bnjkd->bnqkgd', probs, vb)
    return out.reshape(B, S, N_Q_HEADS * HEAD_DIM)


def conformer_conv(u, conv_w, conv_b, ln_g, ln_b):
    a, b = jnp.split(u, 2, axis=-1)
    h = a * jax.nn.sigmoid(b)
    h = jnp.pad(h, ((0, 0), (CONV_KERNEL - 1, 0), (0, 0)))
    y = lax.conv_general_dilated(h, conv_w[:, None, :], window_strides=(1,), padding='VALID',
                                 dimension_numbers=('NWC', 'WIO', 'NWC'),
                                 feature_group_count=CONV_WIDTH)
    y = y + conv_b
    y = layer_norm(y, ln_g, ln_b)
    return jax.nn.silu(y)


def route(h, w_router, router_bias):
    T = h.shape[0]
    scores = jax.nn.sigmoid((h @ w_router).astype(jnp.float32))
    sel = scores + router_bias.astype(jnp.float32)
    per_group = N_EXPERTS // N_EXPERT_GROUPS
    grp = sel.reshape(T, N_EXPERT_GROUPS, per_group)
    grp_score = jnp.sum(lax.top_k(grp, 2)[0], axis=-1)
    _, gidx = lax.top_k(grp_score, TOPK_EXPERT_GROUPS)
    gmask = jnp.sum(jax.nn.one_hot(gidx, N_EXPERT_GROUPS, dtype=jnp.float32), axis=1) > 0
    emask = jnp.repeat(gmask, per_group, axis=1)
    _, eidx = lax.top_k(jnp.where(emask, sel, -jnp.inf), TOP_K)
    w = jnp.take_along_axis(scores, eidx, axis=-1)
    w = w / jnp.sum(w, axis=-1, keepdims=True) * ROUTED_SCALE
    return eidx, w


def routed_experts(h, eidx, w, w_gate, w_up, w_down):
    T, D = h.shape
    A = T * TOP_K
    e_flat = eidx.reshape(-1)
    tok = jnp.repeat(jnp.arange(T, dtype=jnp.int32), TOP_K)
    w_flat = w.reshape(-1)
    order = jnp.argsort(e_flat)
    e_sorted = e_flat[order]
    counts = jnp.bincount(e_flat, length=N_EXPERTS)
    padded = (counts + MOE_BLOCK - 1) // MOE_BLOCK * MOE_BLOCK
    pad_end = jnp.cumsum(padded)
    pad_start = pad_end - padded
    start = jnp.cumsum(counts) - counts
    dest = pad_start[e_sorted] + jnp.arange(A, dtype=jnp.int32) - start[e_sorted]
    n_blocks = -(-(A + N_EXPERTS * (MOE_BLOCK - 1)) // MOE_BLOCK)
    n_rows = n_blocks * MOE_BLOCK
    row_tok = jnp.full((n_rows,), T, dtype=jnp.int32).at[dest].set(tok[order])
    row_w = jnp.zeros((n_rows,), h.dtype).at[dest].set(w_flat[order].astype(h.dtype))
    block_start = jnp.arange(n_blocks, dtype=jnp.int32) * MOE_BLOCK
    block_exp = jnp.minimum(jnp.searchsorted(pad_end, block_start, side='right'), N_EXPERTS - 1)
    h_ext = jnp.concatenate([h, jnp.zeros((1, D), h.dtype)], axis=0)

    def expert_block(args):
        rows, e = args
        xb = h_ext[rows]
        return (jax.nn.silu(xb @ w_gate[e]) * (xb @ w_up[e])) @ w_down[e]

    y = lax.map(expert_block, (row_tok.reshape(n_blocks, MOE_BLOCK), block_exp))
    y = y.reshape(n_rows, D) * row_w[:, None]
    return jax.ops.segment_sum(y, row_tok, num_segments=T + 1)[:T]


def setup_inputs(seed: int = 0) -> dict:
    key = jax.random.key(seed)
    ks = jax.random.split(key, 24)
    L, D, E, F = DEPTH, D_MODEL, N_EXPERTS, D_EXPERT
    nrm = lambda k, shape, s: jax.random.normal(k, shape, jnp.float32) * s
    return {
        "x": nrm(ks[0], (BATCH, SEQ, D), 1.0),
        "c": nrm(ks[1], (BATCH, D), 1.0),
        "w_ada": nrm(ks[2], (L, D, 6 * D), 0.1 * D ** -0.5),
        "b_ada": nrm(ks[3], (L, 6 * D), 0.02),
        "w_in": nrm(ks[4], (L, D, IN_COLS), D ** -0.5),
        "sinks": nrm(ks[5], (L, N_Q_HEADS), 0.5),
        "conv_w": nrm(ks[6], (L, CONV_KERNEL, CONV_WIDTH), CONV_KERNEL ** -0.5),
        "conv_b": nrm(ks[7], (L, CONV_WIDTH), 0.02),
        "conv_ln_g": 1.0 + nrm(ks[8], (L, CONV_WIDTH), 0.02),
        "conv_ln_b": nrm(ks[9], (L, CONV_WIDTH), 0.02),
        "w_o": nrm(ks[10], (L, D, D), DEEPNORM_BETA * D ** -0.5),
        "ln1_g": 1.0 + nrm(ks[11], (L, D), 0.02),
        "ln1_b": nrm(ks[12], (L, D), 0.02),
        "w_router": nrm(ks[13], (L, D, E), D ** -0.5),
        "router_bias": nrm(ks[14], (L, E), 0.01),
        "w_gate_e": nrm(ks[15], (L, E, D, F), D ** -0.5),
        "w_up_e": nrm(ks[16], (L, E, D, F), D ** -0.5),
        "w_down_e": nrm(ks[17], (L, E, F, D), DEEPNORM_BETA * F ** -0.5),
        "w_gate_s": nrm(ks[18], (L, D, D_SHARED), D ** -0.5),
        "w_up_s": nrm(ks[19], (L, D, D_SHARED), D ** -0.5),
        "w_down_s": nrm(ks[20], (L, D_SHARED, D), DEEPNORM_BETA * D_SHARED ** -0.5),
        "ln2_g": 1.0 + nrm(ks[21], (L, D), 0.02),
        "ln2_b": nrm(ks[22], (L, D), 0.02),
    }


def reference(x, c, w_ada, b_ada, w_in, sinks, conv_w, conv_b, conv_ln_g, conv_ln_b, w_o,
              ln1_g, ln1_b, w_router, router_bias, w_gate_e, w_up_e, w_down_e,
              w_gate_s, w_up_s, w_down_s, ln2_g, ln2_b):
    B, S, D = x.shape
    c_act = jax.nn.silu(c)
    for l in range(DEPTH):
        mod = c_act @ w_ada[l] + b_ada[l]
        sh1, sc1, g1, sh2, sc2, g2 = [m[:, None, :] for m in jnp.split(mod, 6, axis=-1)]

        h = x * (1.0 + sc1) + sh1
        proj = h @ w_in[l]
        q, k, v, u = jnp.split(proj, [ATTN_WIDTH, ATTN_WIDTH + KV_WIDTH,
                                      ATTN_WIDTH + 2 * KV_WIDTH], axis=-1)
        q = q.reshape(B, S, N_Q_HEADS, HEAD_DIM)
        k = k.reshape(B, S, N_KV_HEADS, HEAD_DIM)
        v = v.reshape(B, S, N_KV_HEADS, HEAD_DIM)
        attn_out = sliding_window_attention(q, k, v, sinks[l])
        conv_out = conformer_conv(u, conv_w[l], conv_b[l], conv_ln_g[l], conv_ln_b[l])
        mix = jnp.concatenate([attn_out, conv_out], axis=-1) @ w_o[l]
        x = layer_norm(DEEPNORM_ALPHA * x + (1.0 + g1) * mix, ln1_g[l], ln1_b[l])

        h = (x * (1.0 + sc2) + sh2).reshape(B * S, D)
        shared = (jax.nn.silu(h @ w_gate_s[l]) * (h @ w_up_s[l])) @ w_down_s[l]
        eidx, ew = route(h, w_router[l], router_bias[l])
        routed = routed_experts(h, eidx, ew, w_gate_e[l], w_up_e[l], w_down_e[l])
        ffn = (shared + routed).reshape(B, S, D)
        x = layer_norm(DEEPNORM_ALPHA * x + (1.0 + g2) * ffn, ln2_g[l], ln2_b[l])
    return x
```

```python
import functools
import math

import jax
import jax.numpy as jnp
from jax import lax
from jax.experimental import pallas as pl
from jax.experimental.pallas import tpu as pltpu

F32 = jnp.float32
BF16 = jnp.bfloat16
NEG_INF = float("-inf")

HEAD_DIM = 64
WINDOW = 128
CONV_KERNEL = 31
CONV_HIST = 32
TOP_K = 8
N_EXPERT_GROUPS = 8
TOPK_EXPERT_GROUPS = 4
ROUTED_SCALE = 2.5
LN_EPS = 1e-5

LANES = 128
SEQ_TILE = 512
TOK_TILE = 512
FIN_TILE = 256
MOE_BLOCK = 128
GATHER_ROWS = 1024
CONV_ROWS = 64
VMEM_LIMIT = 56 * 1024 * 1024


def _alibi_slopes(n_heads):
    return [2.0 ** (-8.0 * (i + 1) / n_heads) for i in range(n_heads)]


def _layer_norm_rows(z, g, b):
    mu = jnp.mean(z, axis=-1, keepdims=True)
    d = z - mu
    var = jnp.mean(d * d, axis=-1, keepdims=True)
    return d * lax.rsqrt(var + LN_EPS) * g + b


def _ada_kernel(c_ref, w_ref, b_ref, o_ref):
    c = c_ref[...]
    ca = (c * jax.nn.sigmoid(c)).astype(BF16)
    o_ref[...] = jnp.dot(ca, w_ref[...].astype(BF16), preferred_element_type=F32) + b_ref[...]


def _ada(c, w_ada, b_ada):
    bsz, d = c.shape
    n_out = w_ada.shape[1]
    return pl.pallas_call(
        _ada_kernel,
        grid=(n_out // d,),
        in_specs=[
            pl.BlockSpec((bsz, d), lambda j: (0, 0)),
            pl.BlockSpec((d, d), lambda j: (0, j)),
            pl.BlockSpec((1, d), lambda j: (0, j)),
        ],
        out_specs=pl.BlockSpec((bsz, d), lambda j: (0, j)),
        out_shape=jax.ShapeDtypeStruct((bsz, n_out), F32),
        compiler_params=pltpu.CompilerParams(vmem_limit_bytes=VMEM_LIMIT),
        name="ada",
    )(c, w_ada, b_ada.reshape(1, n_out))


def _mix_kernel(sinks_ref, x_ref, mod_ref, win_ref, convw_ref, convp_ref, wo_ref, ln_ref, o_ref,
                q_ref, ke_ref, ve_ref, glu_ref, cat_ref, *, ts, aw, alpha):
    s_idx = pl.program_id(1)
    n_heads = aw // HEAD_DIM
    slopes = _alibi_slopes(n_heads)
    cw = cat_ref.shape[1] - aw

    @pl.when(s_idx == 0)
    def _():
        ke_ref[:, 0:WINDOW, :] = jnp.zeros((4, WINDOW, LANES), BF16)
        ve_ref[:, 0:WINDOW, :] = jnp.zeros((4, WINDOW, LANES), BF16)
        glu_ref[0:CONV_HIST, :] = jnp.zeros((CONV_HIST, cw), F32)

    @pl.when(s_idx > 0)
    def _():
        ke_ref[:, 0:WINDOW, :] = ke_ref[:, ts:ts + WINDOW, :]
        ve_ref[:, 0:WINDOW, :] = ve_ref[:, ts:ts + WINDOW, :]
        glu_ref[0:CONV_HIST, :] = glu_ref[ts:ts + CONV_HIST, :]

    x = x_ref[...]
    h = (x * (1.0 + mod_ref[1:2, :]) + mod_ref[0:1, :]).astype(BF16)

    q = jnp.dot(h, win_ref[:, 0:aw], preferred_element_type=F32)
    q_ref[...] = (q * (1.0 / math.sqrt(HEAD_DIM))).astype(BF16)
    kv = jnp.dot(h, win_ref[:, aw:aw + 2 * LANES], preferred_element_type=F32)
    lo = lax.broadcasted_iota(jnp.int32, (ts, LANES), 1) < HEAD_DIM
    for dst, t in ((ke_ref, kv[:, 0:LANES]), (ve_ref, kv[:, LANES:2 * LANES])):
        t_r = pltpu.roll(t, HEAD_DIM, axis=1)
        dst[0, WINDOW:WINDOW + ts, :] = jnp.where(lo, t, 0.0).astype(BF16)
        dst[1, WINDOW:WINDOW + ts, :] = jnp.where(lo, 0.0, t_r).astype(BF16)
        dst[2, WINDOW:WINDOW + ts, :] = jnp.where(lo, t_r, 0.0).astype(BF16)
        dst[3, WINDOW:WINDOW + ts, :] = jnp.where(lo, 0.0, t).astype(BF16)
    u0 = aw + 2 * LANES
    ga = jnp.dot(h, win_ref[:, u0:u0 + cw], preferred_element_type=F32)
    gb = jnp.dot(h, win_ref[:, u0 + cw:u0 + 2 * cw], preferred_element_type=F32)
    glu_ref[CONV_HIST:CONV_HIST + ts, :] = ga * jax.nn.sigmoid(gb)

    qi = lax.broadcasted_iota(jnp.int32, (WINDOW, 2 * WINDOW), 0)
    kj = lax.broadcasted_iota(jnp.int32, (WINDOW, 2 * WINDOW), 1)
    dist = WINDOW + qi - kj
    band = (dist >= 0) & (dist < WINDOW)
    neg_dist = -dist.astype(F32)
    bias_any = jnp.where(band, neg_dist, NEG_INF)
    bias_first = jnp.where(band & ((kj >= WINDOW) | (s_idx > 0)), neg_dist, NEG_INF)
    for i in range(ts // WINDOW):
        r0 = i * WINDOW
        bias = bias_first if i == 0 else bias_any
        for pair in range(aw // LANES):
            g = (2 * pair) // (n_heads // 2)
            qp = q_ref[r0:r0 + WINDOW, pair * LANES:(pair + 1) * LANES]
            out_pair = None
            for par in range(2):
                hd = 2 * pair + par
                kk = ke_ref[2 * g + par, r0:r0 + 2 * WINDOW, :]
                s = lax.dot_general(qp, kk, (((1,), (1,)), ((), ())), preferred_element_type=F32)
                s = s + slopes[hd] * bias
                sink = sinks_ref[hd]
                m = jnp.maximum(jnp.max(s, axis=-1, keepdims=True), sink)
                p = jnp.exp(s - m)
                denom = jnp.sum(p, axis=-1, keepdims=True) + jnp.exp(sink - m)
                vv = ve_ref[2 * g + par, r0:r0 + 2 * WINDOW, :]
                o = jnp.dot(p.astype(BF16), vv, preferred_element_type=F32) * (1.0 / denom)
                out_pair = o if out_pair is None else out_pair + o
            cat_ref[r0:r0 + WINDOW, pair * LANES:(pair + 1) * LANES] = out_pair.astype(BF16)

    conv_b = convp_ref[0:1, :]
    cln_g = convp_ref[1:2, :]
    cln_b = convp_ref[2:3, :]
    off = CONV_HIST - (CONV_KERNEL - 1)
    for c in range(ts // CONV_ROWS):
        c0 = c * CONV_ROWS
        acc = jnp.broadcast_to(conv_b, (CONV_ROWS, cw))
        for j in range(CONV_KERNEL):
            acc = acc + glu_ref[c0 + off + j:c0 + off + j + CONV_ROWS, :] * convw_ref[j:j + 1, :]
        yn = _layer_norm_rows(acc, cln_g, cln_b)
        cat_ref[c0:c0 + CONV_ROWS, aw:aw + cw] = (yn * jax.nn.sigmoid(yn)).astype(BF16)

    mix = jnp.dot(cat_ref[...], wo_ref[...], preferred_element_type=F32)
    z = alpha * x + (1.0 + mod_ref[2:3, :]) * mix
    o_ref[...] = _layer_norm_rows(z, ln_ref[0:1, :], ln_ref[1:2, :])


def _mix(x, mod, w_in, sinks, conv_w, conv_p, w_o, ln, alpha):
    bsz, seq, d = x.shape
    cw = conv_w.shape[1]
    aw = d - cw
    ts = min(SEQ_TILE, seq)
    assert seq % ts == 0 and ts % WINDOW == 0 and aw % LANES == 0
    assert (aw // HEAD_DIM) // 4 == 2, "kernel packs exactly two KV heads into one lane group"
    assert w_in.shape[1] == aw + 2 * LANES + 2 * cw
    kern = functools.partial(_mix_kernel, ts=ts, aw=aw, alpha=alpha)
    const = lambda b, s: (0, 0)
    return pl.pallas_call(
        kern,
        grid=(bsz, seq // ts),
        in_specs=[
            pl.BlockSpec(memory_space=pltpu.SMEM),
            pl.BlockSpec((None, ts, d), lambda b, s: (b, s, 0)),
            pl.BlockSpec((None, 6, d), lambda b, s: (b, 0, 0)),
            pl.BlockSpec(w_in.shape, const),
            pl.BlockSpec(conv_w.shape, const),
            pl.BlockSpec(conv_p.shape, const),
            pl.BlockSpec(w_o.shape, const),
            pl.BlockSpec(ln.shape, const),
        ],
        out_specs=pl.BlockSpec((None, ts, d), lambda b, s: (b, s, 0)),
        out_shape=jax.ShapeDtypeStruct((bsz, seq, d), F32),
        scratch_shapes=[
            pltpu.VMEM((ts, aw), BF16),
            pltpu.VMEM((4, ts + WINDOW, LANES), BF16),
            pltpu.VMEM((4, ts + WINDOW, LANES), BF16),
            pltpu.VMEM((ts + CONV_HIST, cw), F32),
            pltpu.VMEM((ts, d), BF16),
        ],
        compiler_params=pltpu.CompilerParams(
            dimension_semantics=("arbitrary", "arbitrary"), vmem_limit_bytes=VMEM_LIMIT),
        name="mix",
    )(sinks, x, mod, w_in, conv_w, conv_p, w_o, ln)


def _ffn_pre_kernel(x_ref, mod_ref, wgs_ref, wus_ref, wds_ref, wrt_ref, rb_ref,
                    h_ref, sh_ref, ei_ref, ew_ref):
    tm = x_ref.shape[0]
    n_exp = wrt_ref.shape[0]
    per = n_exp // N_EXPERT_GROUPS
    hf = x_ref[...] * (1.0 + mod_ref[4:5, :]) + mod_ref[3:4, :]
    h_ref[...] = hf
    h = hf.astype(BF16)

    gate = jnp.dot(h, wgs_ref[...], preferred_element_type=F32)
    up = jnp.dot(h, wus_ref[...], preferred_element_type=F32)
    act = (gate * jax.nn.sigmoid(gate) * up).astype(BF16)
    sh_ref[...] = jnp.dot(act, wds_ref[...], preferred_element_type=F32)

    logits = lax.dot_general(wrt_ref[...], h, (((1,), (1,)), ((), ())), preferred_element_type=F32)
    scores = jax.nn.sigmoid(logits)
    sel = scores + rb_ref[...]

    iota_p = lax.broadcasted_iota(jnp.int32, (per, tm), 0).astype(F32)
    gs_rows = []
    for g in range(N_EXPERT_GROUPS):
        blk = sel[g * per:(g + 1) * per, :]
        m1 = jnp.max(blk, axis=0, keepdims=True)
        i1 = jnp.min(jnp.where(blk == m1, iota_p, float(per)), axis=0, keepdims=True)
        m2 = jnp.max(jnp.where(iota_p == i1, NEG_INF, blk), axis=0, keepdims=True)
        gs_rows.append(m1 + m2)
    gs = jnp.concatenate(gs_rows, axis=0)
    iota_g = lax.broadcasted_iota(jnp.int32, (N_EXPERT_GROUPS, tm), 0).astype(F32)
    gmask = jnp.zeros((N_EXPERT_GROUPS, tm), jnp.bool_)
    for _ in range(TOPK_EXPERT_GROUPS):
        m = jnp.max(gs, axis=0, keepdims=True)
        gi = jnp.min(jnp.where(gs == m, iota_g, float(N_EXPERT_GROUPS)), axis=0, keepdims=True)
        hit = iota_g == gi
        gmask = gmask | hit
        gs = jnp.where(hit, NEG_INF, gs)
    emask = jnp.concatenate(
        [jnp.broadcast_to(gmask[g:g + 1, :], (per, tm)) for g in range(N_EXPERT_GROUPS)], axis=0)
    cand = jnp.where(emask, sel, NEG_INF)

    iota_e = lax.broadcasted_iota(jnp.int32, (n_exp, tm), 0).astype(F32)
    idx_rows, w_rows = [], []
    for _ in range(TOP_K):
        m = jnp.max(cand, axis=0, keepdims=True)
        ei = jnp.min(jnp.where(cand == m, iota_e, float(n_exp)), axis=0, keepdims=True)
        hit = iota_e == ei
        w_rows.append(jnp.sum(jnp.where(hit, scores, 0.0), axis=0, keepdims=True))
        idx_rows.append(ei)
        cand = jnp.where(hit, NEG_INF, cand)
    wk = jnp.concatenate(w_rows, axis=0)
    ew_ref[...] = wk / jnp.sum(wk, axis=0, keepdims=True) * ROUTED_SCALE
    ei_ref[...] = jnp.concatenate(idx_rows, axis=0).astype(jnp.int32)


def _ffn_pre(x1, mod, wgs, wus, wds, wrt, rbias, seq):
    n_tok, d = x1.shape
    tm = min(TOK_TILE, seq)
    assert seq % tm == 0 and n_tok % tm == 0
    n_exp = wrt.shape[0]
    const = lambda i: (0, 0)
    return pl.pallas_call(
        _ffn_pre_kernel,
        grid=(n_tok // tm,),
        in_specs=[
            pl.BlockSpec((tm, d), lambda i: (i, 0)),
            pl.BlockSpec((None, 6, d), lambda i: ((i * tm) // seq, 0, 0)),
            pl.BlockSpec(wgs.shape, const),
            pl.BlockSpec(wus.shape, const),
            pl.BlockSpec(wds.shape, const),
            pl.BlockSpec(wrt.shape, const),
            pl.BlockSpec(rbias.shape, const),
        ],
        out_specs=[
            pl.BlockSpec((tm, d), lambda i: (i, 0)),
            pl.BlockSpec((tm, d), lambda i: (i, 0)),
            pl.BlockSpec((TOP_K, tm), lambda i: (0, i)),
            pl.BlockSpec((TOP_K, tm), lambda i: (0, i)),
        ],
        out_shape=[
            jax.ShapeDtypeStruct((n_tok, d), F32),
            jax.ShapeDtypeStruct((n_tok, d), F32),
            jax.ShapeDtypeStruct((TOP_K, n_tok), jnp.int32),
            jax.ShapeDtypeStruct((TOP_K, n_tok), F32),
        ],
        compiler_params=pltpu.CompilerParams(
            dimension_semantics=("arbitrary",), vmem_limit_bytes=VMEM_LIMIT),
        name="ffn_pre",
    )(x1, mod, wgs, wus, wds, wrt, rbias)


def _gather_kernel(idx_ref, table_ref, out_ref, sem):
    rows = idx_ref.shape[2]
    base = pl.program_id(0) * rows

    def row_copy(r):
        return pltpu.make_async_copy(
            table_ref.at[pl.ds(idx_ref[0, 0, r], 1)], out_ref.at[pl.ds(base + r, 1)], sem)

    def start(r, carry):
        row_copy(r).start()
        return carry

    def wait(r, carry):
        row_copy(r).wait()
        return carry

    lax.fori_loop(0, rows, start, 0)
    lax.fori_loop(0, rows, wait, 0)


def _gather_rows(table, idx):
    n_out = idx.shape[0]
    rows = min(GATHER_ROWS, n_out)
    assert n_out % rows == 0
    return pl.pallas_call(
        _gather_kernel,
        grid=(n_out // rows,),
        in_specs=[
            pl.BlockSpec((1, 1, rows), lambda i: (i, 0, 0), memory_space=pltpu.SMEM),
            pl.BlockSpec(memory_space=pl.ANY),
        ],
        out_specs=pl.BlockSpec(memory_space=pl.ANY),
        out_shape=jax.ShapeDtypeStruct((n_out,) + table.shape[1:], table.dtype),
        scratch_shapes=[pltpu.SemaphoreType.DMA],
        compiler_params=pltpu.CompilerParams(dimension_semantics=("arbitrary",)),
        name="gather_rows",
    )(idx.reshape(n_out // rows, 1, rows), table)


def _expert_kernel(bexp_ref, nused_ref, x_ref, wg_ref, wu_ref, wd_ref, y_ref):
    del bexp_ref

    @pl.when(pl.program_id(0) < nused_ref[0])
    def _():
        x = x_ref[...].astype(BF16)
        gate = jnp.dot(x, wg_ref[...].astype(BF16), preferred_element_type=F32)
        up = jnp.dot(x, wu_ref[...].astype(BF16), preferred_element_type=F32)
        act = (gate * jax.nn.sigmoid(gate) * up).astype(BF16)
        y_ref[...] = jnp.dot(act, wd_ref[...].astype(BF16), preferred_element_type=F32)


def _experts(xs, block_exp, n_used, w_gate, w_up, w_down, n_blocks):
    d = xs.shape[1]
    f = w_gate.shape[2]

    def row_map(b, bexp, nused):
        return (jnp.minimum(b, nused[0] - 1), 0)

    def w_map(b, bexp, nused):
        return (bexp[jnp.minimum(b, nused[0] - 1)], 0, 0)

    return pl.pallas_call(
        _expert_kernel,
        grid_spec=pltpu.PrefetchScalarGridSpec(
            num_scalar_prefetch=2,
            grid=(n_blocks,),
            in_specs=[
                pl.BlockSpec((MOE_BLOCK, d), row_map),
                pl.BlockSpec((None, d, f), w_map),
                pl.BlockSpec((None, d, f), w_map),
                pl.BlockSpec((None, f, d), w_map),
            ],
            out_specs=pl.BlockSpec((MOE_BLOCK, d), row_map),
        ),
        out_shape=jax.ShapeDtypeStruct((n_blocks * MOE_BLOCK, d), F32),
        compiler_params=pltpu.CompilerParams(
            dimension_semantics=("arbitrary",), vmem_limit_bytes=VMEM_LIMIT),
        name="experts",
    )(block_exp, n_used, xs, w_gate, w_up, w_down)


def _final_kernel(x_ref, sh_ref, yg_ref, w_ref, mod_ref, ln_ref, o_ref, *, alpha):
    ffn = sh_ref[...]
    for k in range(TOP_K):
        ffn = ffn + yg_ref[k] * w_ref[:, k:k + 1]
    z = alpha * x_ref[...] + (1.0 + mod_ref[5:6, :]) * ffn
    o_ref[...] = _layer_norm_rows(z, ln_ref[0:1, :], ln_ref[1:2, :])


def _final(x1, shared, yg, w_tok, mod, ln, seq, alpha):
    n_tok, d = x1.shape
    tm = min(FIN_TILE, seq)
    assert seq % tm == 0
    return pl.pallas_call(
        functools.partial(_final_kernel, alpha=alpha),
        grid=(n_tok // tm,),
        in_specs=[
            pl.BlockSpec((tm, d), lambda i: (i, 0)),
            pl.BlockSpec((tm, d), lambda i: (i, 0)),
            pl.BlockSpec((TOP_K, tm, d), lambda i: (0, i, 0)),
            pl.BlockSpec((tm, TOP_K), lambda i: (i, 0)),
            pl.BlockSpec((None, 6, d), lambda i: ((i * tm) // seq, 0, 0)),
            pl.BlockSpec(ln.shape, lambda i: (0, 0)),
        ],
        out_specs=pl.BlockSpec((tm, d), lambda i: (i, 0)),
        out_shape=jax.ShapeDtypeStruct((n_tok, d), F32),
        compiler_params=pltpu.CompilerParams(
            dimension_semantics=("arbitrary",), vmem_limit_bytes=VMEM_LIMIT),
        name="final",
    )(x1, shared, yg, w_tok, mod, ln)


def _dispatch_tables(eidx_t, n_exp, n_blocks, n_rows_padded):
    n_tok = eidx_t.shape[1]
    n_asg = eidx_t.size
    e_flat = eidx_t.reshape(-1)
    tok = jnp.arange(n_asg, dtype=jnp.int32) % n_tok
    order = jnp.argsort(e_flat)
    e_sorted = e_flat[order]
    counts = jnp.bincount(e_flat, length=n_exp).astype(jnp.int32)
    padded = (counts + MOE_BLOCK - 1) // MOE_BLOCK * MOE_BLOCK
    pad_end = jnp.cumsum(padded)
    pad_start = pad_end - padded
    start = jnp.cumsum(counts) - counts
    dest = pad_start[e_sorted] + jnp.arange(n_asg, dtype=jnp.int32) - start[e_sorted]
    row_tok = jnp.zeros((n_rows_padded,), jnp.int32).at[dest].set(tok[order])
    pos = jnp.zeros((n_asg,), jnp.int32).at[order].set(dest)
    block_start = jnp.arange(n_blocks, dtype=jnp.int32) * MOE_BLOCK
    block_exp = jnp.minimum(jnp.searchsorted(pad_end, block_start, side="right"), n_exp - 1)
    n_used = (pad_end[-1] // MOE_BLOCK).astype(jnp.int32).reshape(1)
    return row_tok, pos, block_exp.astype(jnp.int32), n_used


def kernel(x, c, w_ada, b_ada, w_in, sinks, conv_w, conv_b, conv_ln_g, conv_ln_b, w_o, ln1_g, ln1_b,
           w_router, router_bias, w_gate_e, w_up_e, w_down_e, w_gate_s, w_up_s, w_down_s, ln2_g, ln2_b):
    bsz, seq, d = x.shape
    depth = w_ada.shape[0]
    n_exp = w_router.shape[2]
    n_tok = bsz * seq
    n_asg = n_tok * TOP_K
    alpha = (2.0 * depth) ** 0.25
    n_blocks = -(-(n_asg + n_exp * (MOE_BLOCK - 1)) // MOE_BLOCK)
    g_rows = min(GATHER_ROWS, n_asg)
    n_rows_padded = -(-(n_blocks * MOE_BLOCK) // g_rows) * g_rows

    for l in range(depth):
        mod = _ada(c, w_ada[l], b_ada[l]).reshape(bsz, 6, d)
        conv_p = jnp.stack([conv_b[l], conv_ln_g[l], conv_ln_b[l]])
        x1 = _mix(x, mod, w_in[l].astype(BF16), sinks[l], conv_w[l], conv_p, w_o[l].astype(BF16),
                  jnp.stack([ln1_g[l], ln1_b[l]]), alpha)
        x1 = x1.reshape(n_tok, d)
        h2, shared, eidx_t, ew_t = _ffn_pre(
            x1, mod, w_gate_s[l].astype(BF16), w_up_s[l].astype(BF16), w_down_s[l].astype(BF16),
            w_router[l].T.astype(BF16), router_bias[l].reshape(n_exp, 1), seq)
        row_tok, pos, block_exp, n_used = _dispatch_tables(eidx_t, n_exp, n_blocks, n_rows_padded)
        xs = _gather_rows(h2, row_tok)
        y = _experts(xs, block_exp, n_used, w_gate_e[l], w_up_e[l], w_down_e[l], n_blocks)
        yg = _gather_rows(y, pos).reshape(TOP_K, n_tok, d)
        x = _final(x1, shared, yg, ew_t.T, mod, jnp.stack([ln2_g[l], ln2_b[l]]), seq, alpha)
        x = x.reshape(bsz, seq, d)
    return x
```

```python
import functools
import math

import jax
import jax.numpy as jnp
from jax import lax
from jax.experimental import pallas as pl
from jax.experimental.pallas import tpu as pltpu

F32 = jnp.float32
BF16 = jnp.bfloat16
NEG_INF = float("-inf")

HEAD_DIM = 64
WINDOW = 128
CONV_KERNEL = 31
CONV_HIST = 32
TOP_K = 8
N_EXPERT_GROUPS = 8
TOPK_EXPERT_GROUPS = 4
ROUTED_SCALE = 2.5
LN_EPS = 1e-5

LANES = 128
SEQ_TILE = 512
TOK_TILE = 512
FIN_TILE = 256
MOE_BLOCK = 128
GATHER_ROWS = 1024
CONV_ROWS = 64
VMEM_LIMIT = 56 * 1024 * 1024


def _alibi_slopes(n_heads):
    return [2.0 ** (-8.0 * (i + 1) / n_heads) for i in range(n_heads)]


def _layer_norm_rows(z, g, b):
    mu = jnp.mean(z, axis=-1, keepdims=True)
    d = z - mu
    var = jnp.mean(d * d, axis=-1, keepdims=True)
    return d * lax.rsqrt(var + LN_EPS) * g + b


def _ada_kernel(c_ref, w_ref, b_ref, o_ref):
    c = c_ref[...]
    ca = (c * jax.nn.sigmoid(c)).astype(BF16)
    o_ref[...] = jnp.dot(ca, w_ref[...].astype(BF16), preferred_element_type=F32) + b_ref[...]


def _ada(c, w_ada, b_ada):
    bsz, d = c.shape
    n_out = w_ada.shape[1]
    return pl.pallas_call(
        _ada_kernel,
        grid=(n_out // d,),
        in_specs=[
            pl.BlockSpec((bsz, d), lambda j: (0, 0)),
            pl.BlockSpec((d, d), lambda j: (0, j)),
            pl.BlockSpec((1, d), lambda j: (0, j)),
        ],
        out_specs=pl.BlockSpec((bsz, d), lambda j: (0, j)),
        out_shape=jax.ShapeDtypeStruct((bsz, n_out), F32),
        compiler_params=pltpu.CompilerParams(vmem_limit_bytes=VMEM_LIMIT),
        name="ada",
    )(c, w_ada, b_ada.reshape(1, n_out))


def _mix_kernel(sinks_ref, x_ref, mod_ref, win_ref, convw_ref, convp_ref, wo_ref, ln_ref, o_ref,
                q_ref, ke_ref, ve_ref, glu_ref, cat_ref, *, ts, aw, alpha):
    s_idx = pl.program_id(1)
    n_heads = aw // HEAD_DIM
    slopes = _alibi_slopes(n_heads)
    cw = cat_ref.shape[1] - aw

    @pl.when(s_idx == 0)
    def _():
        ke_ref[:, 0:WINDOW, :] = jnp.zeros((4, WINDOW, LANES), BF16)
        ve_ref[:, 0:WINDOW, :] = jnp.zeros((4, WINDOW, LANES), BF16)
        glu_ref[0:CONV_HIST, :] = jnp.zeros((CONV_HIST, cw), F32)

    @pl.when(s_idx > 0)
    def _():
        ke_ref[:, 0:WINDOW, :] = ke_ref[:, ts:ts + WINDOW, :]
        ve_ref[:, 0:WINDOW, :] = ve_ref[:, ts:ts + WINDOW, :]
        glu_ref[0:CONV_HIST, :] = glu_ref[ts:ts + CONV_HIST, :]

    x = x_ref[...]
    h = (x * (1.0 + mod_ref[1:2, :]) + mod_ref[0:1, :]).astype(BF16)

    q = jnp.dot(h, win_ref[:, 0:aw], preferred_element_type=F32)
    q_ref[...] = (q * (1.0 / math.sqrt(HEAD_DIM))).astype(BF16)
    kv = jnp.dot(h, win_ref[:, aw:aw + 2 * LANES], preferred_element_type=F32)
    lo = lax.broadcasted_iota(jnp.int32, (ts, LANES), 1) < HEAD_DIM
    for dst, t in ((ke_ref, kv[:, 0:LANES]), (ve_ref, kv[:, LANES:2 * LANES])):
        t_r = pltpu.roll(t, HEAD_DIM, axis=1)
        dst[0, WINDOW:WINDOW + ts, :] = jnp.where(lo, t, 0.0).astype(BF16)
        dst[1, WINDOW:WINDOW + ts, :] = jnp.where(lo, 0.0, t_r).astype(BF16)
        dst[2, WINDOW:WINDOW + ts, :] = jnp.where(lo, t_r, 0.0).astype(BF16)
        dst[3, WINDOW:WINDOW + ts, :] = jnp.where(lo, 0.0, t).astype(BF16)
    u0 = aw + 2 * LANES
    ga = jnp.dot(h, win_ref[:, u0:u0 + cw], preferred_element_type=F32)
    gb = jnp.dot(h, win_ref[:, u0 + cw:u0 + 2 * cw], preferred_element_type=F32)
    glu_ref[CONV_HIST:CONV_HIST + ts, :] = ga * jax.nn.sigmoid(gb)

    qi = lax.broadcasted_iota(jnp.int32, (WINDOW, 2 * WINDOW), 0)
    kj = lax.broadcasted_iota(jnp.int32, (WINDOW, 2 * WINDOW), 1)
    dist = WINDOW + qi - kj
    band = (dist >= 0) & (dist < WINDOW)
    neg_dist = -dist.astype(F32)
    bias_any = jnp.where(band, neg_dist, NEG_INF)
    bias_first = jnp.where(band & ((kj >= WINDOW) | (s_idx > 0)), neg_dist, NEG_INF)
    for i in range(ts // WINDOW):
        r0 = i * WINDOW
        bias = bias_first if i == 0 else bias_any
        for pair in range(aw // LANES):
            g = (2 * pair) // (n_heads // 2)
            qp = q_ref[r0:r0 + WINDOW, pair * LANES:(pair + 1) * LANES]
            out_pair = None
            for par in range(2):
                hd = 2 * pair + par
                kk = ke_ref[2 * g + par, r0:r0 + 2 * WINDOW, :]
                s = lax.dot_general(qp, kk, (((1,), (1,)), ((), ())), preferred_element_type=F32)
                s = s + slopes[hd] * bias
                sink = sinks_ref[hd]
                m = jnp.maximum(jnp.max(s, axis=-1, keepdims=True), sink)
                p = jnp.exp(s - m)
                denom = jnp.sum(p, axis=-1, keepdims=True) + jnp.exp(sink - m)
                vv = ve_ref[2 * g + par, r0:r0 + 2 * WINDOW, :]
                o = jnp.dot(p.astype(BF16), vv, preferred_element_type=F32) * (1.0 / denom)
                out_pair = o if out_pair is None else out_pair + o
            cat_ref[r0:r0 + WINDOW, pair * LANES:(pair + 1) * LANES] = out_pair.astype(BF16)

    conv_b = convp_ref[0:1, :]
    cln_g = convp_ref[1:2, :]
    cln_b = convp_ref[2:3, :]
    off = CONV_HIST - (CONV_KERNEL - 1)
    for c in range(ts // CONV_ROWS):
        c0 = c * CONV_ROWS
        acc = jnp.broadcast_to(conv_b, (CONV_ROWS, cw))
        for j in range(CONV_KERNEL):
            acc = acc + glu_ref[c0 + off + j:c0 + off + j + CONV_ROWS, :] * convw_ref[j:j + 1, :]
        yn = _layer_norm_rows(acc, cln_g, cln_b)
        cat_ref[c0:c0 + CONV_ROWS, aw:aw + cw] = (yn * jax.nn.sigmoid(yn)).astype(BF16)

    mix = jnp.dot(cat_ref[...], wo_ref[...], preferred_element_type=F32)
    z = alpha * x + (1.0 + mod_ref[2:3, :]) * mix
    o_ref[...] = _layer_norm_rows(z, ln_ref[0:1, :], ln_ref[1:2, :])


def _mix(x, mod, w_in, sinks, conv_w, conv_p, w_o, ln, alpha):
    bsz, seq, d = x.shape
    cw = conv_w.shape[1]
    aw = d - cw
    ts = min(SEQ_TILE, seq)
    assert seq % ts == 0 and ts % WINDOW == 0 and aw % LANES == 0
    assert (aw // HEAD_DIM) // 4 == 2, "kernel packs exactly two KV heads into one lane group"
    assert w_in.shape[1] == aw + 2 * LANES + 2 * cw
    kern = functools.partial(_mix_kernel, ts=ts, aw=aw, alpha=alpha)
    const = lambda b, s: (0, 0)
    return pl.pallas_call(
        kern,
        grid=(bsz, seq // ts),
        in_specs=[
            pl.BlockSpec(memory_space=pltpu.SMEM),
            pl.BlockSpec((None, ts, d), lambda b, s: (b, s, 0)),
            pl.BlockSpec((None, 6, d), lambda b, s: (b, 0, 0)),
            pl.BlockSpec(w_in.shape, const),
            pl.BlockSpec(conv_w.shape, const),
            pl.BlockSpec(conv_p.shape, const),
            pl.BlockSpec(w_o.shape, const),
            pl.BlockSpec(ln.shape, const),
        ],
        out_specs=pl.BlockSpec((None, ts, d), lambda b, s: (b, s, 0)),
        out_shape=jax.ShapeDtypeStruct((bsz, seq, d), F32),
        scratch_shapes=[
            pltpu.VMEM((ts, aw), BF16),
            pltpu.VMEM((4, ts + WINDOW, LANES), BF16),
            pltpu.VMEM((4, ts + WINDOW, LANES), BF16),
            pltpu.VMEM((ts + CONV_HIST, cw), F32),
            pltpu.VMEM((ts, d), BF16),
        ],
        compiler_params=pltpu.CompilerParams(
            dimension_semantics=("arbitrary", "arbitrary"), vmem_limit_bytes=VMEM_LIMIT),
        name="mix",
    )(sinks, x, mod, w_in, conv_w, conv_p, w_o, ln)


def _ffn_pre_kernel(x_ref, mod_ref, wgs_ref, wus_ref, wds_ref, wrt_ref, rb_ref,
                    h_ref, sh_ref, ei_ref, ew_ref):
    tm = x_ref.shape[0]
    n_exp = wrt_ref.shape[0]
    per = n_exp // N_EXPERT_GROUPS
    hf = x_ref[...] * (1.0 + mod_ref[4:5, :]) + mod_ref[3:4, :]
    h_ref[...] = hf
    h = hf.astype(BF16)

    gate = jnp.dot(h, wgs_ref[...], preferred_element_type=F32)
    up = jnp.dot(h, wus_ref[...], preferred_element_type=F32)
    act = (gate * jax.nn.sigmoid(gate) * up).astype(BF16)
    sh_ref[...] = jnp.dot(act, wds_ref[...], preferred_element_type=F32)

    logits = lax.dot_general(wrt_ref[...], h, (((1,), (1,)), ((), ())), preferred_element_type=F32)
    scores = jax.nn.sigmoid(logits)
    sel = scores + rb_ref[...]

    iota_p = lax.broadcasted_iota(jnp.int32, (per, tm), 0).astype(F32)
    gs_rows = []
    for g in range(N_EXPERT_GROUPS):
        blk = sel[g * per:(g + 1) * per, :]
        m1 = jnp.max(blk, axis=0, keepdims=True)
        i1 = jnp.min(jnp.where(blk == m1, iota_p, float(per)), axis=0, keepdims=True)
        m2 = jnp.max(jnp.where(iota_p == i1, NEG_INF, blk), axis=0, keepdims=True)
        gs_rows.append(m1 + m2)
    gs = jnp.concatenate(gs_rows, axis=0)
    iota_g = lax.broadcasted_iota(jnp.int32, (N_EXPERT_GROUPS, tm), 0).astype(F32)
    gmask = jnp.zeros((N_EXPERT_GROUPS, tm), jnp.bool_)
    for _ in range(TOPK_EXPERT_GROUPS):
        m = jnp.max(gs, axis=0, keepdims=True)
        gi = jnp.min(jnp.where(gs == m, iota_g, float(N_EXPERT_GROUPS)), axis=0, keepdims=True)
        hit = iota_g == gi
        gmask = gmask | hit
        gs = jnp.where(hit, NEG_INF, gs)
    emask = jnp.concatenate(
        [jnp.broadcast_to(gmask[g:g + 1, :], (per, tm)) for g in range(N_EXPERT_GROUPS)], axis=0)
    cand = jnp.where(emask, sel, NEG_INF)

    iota_e = lax.broadcasted_iota(jnp.int32, (n_exp, tm), 0).astype(F32)
    idx_rows, w_rows = [], []
    for _ in range(TOP_K):
        m = jnp.max(cand, axis=0, keepdims=True)
        ei = jnp.min(jnp.where(cand == m, iota_e, float(n_exp)), axis=0, keepdims=True)
        hit = iota_e == ei
        w_rows.append(jnp.sum(jnp.where(hit, scores, 0.0), axis=0, keepdims=True))
        idx_rows.append(ei)
        cand = jnp.where(hit, NEG_INF, cand)
    wk = jnp.concatenate(w_rows, axis=0)
    ew_ref[...] = wk / jnp.sum(wk, axis=0, keepdims=True) * ROUTED_SCALE
    ei_ref[...] = jnp.concatenate(idx_rows, axis=0).astype(jnp.int32)


def _ffn_pre(x1, mod, wgs, wus, wds, wrt, rbias, seq):
    n_tok, d = x1.shape
    tm = min(TOK_TILE, seq)
    assert seq % tm == 0 and n_tok % tm == 0
    n_exp = wrt.shape[0]
    const = lambda i: (0, 0)
    return pl.pallas_call(
        _ffn_pre_kernel,
        grid=(n_tok // tm,),
        in_specs=[
            pl.BlockSpec((tm, d), lambda i: (i, 0)),
            pl.BlockSpec((None, 6, d), lambda i: ((i * tm) // seq, 0, 0)),
            pl.BlockSpec(wgs.shape, const),
            pl.BlockSpec(wus.shape, const),
            pl.BlockSpec(wds.shape, const),
            pl.BlockSpec(wrt.shape, const),
            pl.BlockSpec(rbias.shape, const),
        ],
        out_specs=[
            pl.BlockSpec((tm, d), lambda i: (i, 0)),
            pl.BlockSpec((tm, d), lambda i: (i, 0)),
            pl.BlockSpec((TOP_K, tm), lambda i: (0, i)),
            pl.BlockSpec((TOP_K, tm), lambda i: (0, i)),
        ],
        out_shape=[
            jax.ShapeDtypeStruct((n_tok, d), F32),
            jax.ShapeDtypeStruct((n_tok, d), F32),
            jax.ShapeDtypeStruct((TOP_K, n_tok), jnp.int32),
            jax.ShapeDtypeStruct((TOP_K, n_tok), F32),
        ],
        compiler_params=pltpu.CompilerParams(
            dimension_semantics=("arbitrary",), vmem_limit_bytes=VMEM_LIMIT),
        name="ffn_pre",
    )(x1, mod, wgs, wus, wds, wrt, rbias)


def _gather_kernel(idx_ref, table_ref, out_ref, sem):
    rows = out_ref.shape[0]

    def row_copy(r):
        return pltpu.make_async_copy(
            table_ref.at[pl.ds(idx_ref[0, 0, r], 1)], out_ref.at[pl.ds(r, 1)], sem)

    def start(r, carry):
        row_copy(r).start()
        return carry

    def wait(r, carry):
        row_copy(r).wait()
        return carry

    lax.fori_loop(0, rows, start, 0)
    lax.fori_loop(0, rows, wait, 0)


def _gather_rows(table, idx):
    n_out = idx.shape[0]
    d = table.shape[1]
    rows = min(GATHER_ROWS, n_out)
    assert n_out % rows == 0
    return pl.pallas_call(
        _gather_kernel,
        grid=(n_out // rows,),
        in_specs=[
            pl.BlockSpec((1, 1, rows), lambda i: (i, 0, 0), memory_space=pltpu.SMEM),
            pl.BlockSpec(memory_space=pl.ANY),
        ],
        out_specs=pl.BlockSpec((rows, d), lambda i: (i, 0)),
        out_shape=jax.ShapeDtypeStruct((n_out, d), table.dtype),
        scratch_shapes=[pltpu.SemaphoreType.DMA],
        compiler_params=pltpu.CompilerParams(
            dimension_semantics=("arbitrary",), vmem_limit_bytes=VMEM_LIMIT),
        name="gather_rows",
    )(idx.reshape(n_out // rows, 1, rows), table)


def _expert_kernel(bexp_ref, nused_ref, x_ref, wg_ref, wu_ref, wd_ref, y_ref):
    del bexp_ref

    @pl.when(pl.program_id(0) < nused_ref[0])
    def _():
        x = x_ref[...].astype(BF16)
        gate = jnp.dot(x, wg_ref[...].astype(BF16), preferred_element_type=F32)
        up = jnp.dot(x, wu_ref[...].astype(BF16), preferred_element_type=F32)
        act = (gate * jax.nn.sigmoid(gate) * up).astype(BF16)
        y_ref[...] = jnp.dot(act, wd_ref[...].astype(BF16), preferred_element_type=F32)


def _experts(xs, block_exp, n_used, w_gate, w_up, w_down, n_blocks):
    d = xs.shape[1]
    f = w_gate.shape[2]

    def row_map(b, bexp, nused):
        return (jnp.minimum(b, nused[0] - 1), 0)

    def w_map(b, bexp, nused):
        return (bexp[jnp.minimum(b, nused[0] - 1)], 0, 0)

    return pl.pallas_call(
        _expert_kernel,
        grid_spec=pltpu.PrefetchScalarGridSpec(
            num_scalar_prefetch=2,
            grid=(n_blocks,),
            in_specs=[
                pl.BlockSpec((MOE_BLOCK, d), row_map),
                pl.BlockSpec((None, d, f), w_map),
                pl.BlockSpec((None, d, f), w_map),
                pl.BlockSpec((None, f, d), w_map),
            ],
            out_specs=pl.BlockSpec((MOE_BLOCK, d), row_map),
        ),
        out_shape=jax.ShapeDtypeStruct((n_blocks * MOE_BLOCK, d), F32),
        compiler_params=pltpu.CompilerParams(
            dimension_semantics=("arbitrary",), vmem_limit_bytes=VMEM_LIMIT),
        name="experts",
    )(block_exp, n_used, xs, w_gate, w_up, w_down)


def _final_kernel(x_ref, sh_ref, yg_ref, w_ref, mod_ref, ln_ref, o_ref, *, alpha):
    ffn = sh_ref[...]
    for k in range(TOP_K):
        ffn = ffn + yg_ref[k] * w_ref[:, k:k + 1]
    z = alpha * x_ref[...] + (1.0 + mod_ref[5:6, :]) * ffn
    o_ref[...] = _layer_norm_rows(z, ln_ref[0:1, :], ln_ref[1:2, :])


def _final(x1, shared, yg, w_tok, mod, ln, seq, alpha):
    n_tok, d = x1.shape
    tm = min(FIN_TILE, seq)
    assert seq % tm == 0
    return pl.pallas_call(
        functools.partial(_final_kernel, alpha=alpha),
        grid=(n_tok // tm,),
        in_specs=[
            pl.BlockSpec((tm, d), lambda i: (i, 0)),
            pl.BlockSpec((tm, d), lambda i: (i, 0)),
            pl.BlockSpec((TOP_K, tm, d), lambda i: (0, i, 0)),
            pl.BlockSpec((tm, TOP_K), lambda i: (i, 0)),
            pl.BlockSpec((None, 6, d), lambda i: ((i * tm) // seq, 0, 0)),
            pl.BlockSpec(ln.shape, lambda i: (0, 0)),
        ],
        out_specs=pl.BlockSpec((tm, d), lambda i: (i, 0)),
        out_shape=jax.ShapeDtypeStruct((n_tok, d), F32),
        compiler_params=pltpu.CompilerParams(
            dimension_semantics=("arbitrary",), vmem_limit_bytes=VMEM_LIMIT),
        name="final",
    )(x1, shared, yg, w_tok, mod, ln)


def _dispatch_tables(eidx_t, n_exp, n_blocks, n_rows_padded):
    n_tok = eidx_t.shape[1]
    n_asg = eidx_t.size
    e_flat = eidx_t.reshape(-1)
    tok = jnp.arange(n_asg, dtype=jnp.int32) % n_tok
    order = jnp.argsort(e_flat)
    e_sorted = e_flat[order]
    counts = jnp.bincount(e_flat, length=n_exp).astype(jnp.int32)
    padded = (counts + MOE_BLOCK - 1) // MOE_BLOCK * MOE_BLOCK
    pad_end = jnp.cumsum(padded)
    pad_start = pad_end - padded
    start = jnp.cumsum(counts) - counts
    dest = pad_start[e_sorted] + jnp.arange(n_asg, dtype=jnp.int32) - start[e_sorted]
    row_tok = jnp.zeros((n_rows_padded,), jnp.int32).at[dest].set(tok[order])
    pos = jnp.zeros((n_asg,), jnp.int32).at[order].set(dest)
    block_start = jnp.arange(n_blocks, dtype=jnp.int32) * MOE_BLOCK
    block_exp = jnp.minimum(jnp.searchsorted(pad_end, block_start, side="right"), n_exp - 1)
    n_used = (pad_end[-1] // MOE_BLOCK).astype(jnp.int32).reshape(1)
    return row_tok, pos, block_exp.astype(jnp.int32), n_used


def kernel(x, c, w_ada, b_ada, w_in, sinks, conv_w, conv_b, conv_ln_g, conv_ln_b, w_o, ln1_g, ln1_b,
           w_router, router_bias, w_gate_e, w_up_e, w_down_e, w_gate_s, w_up_s, w_down_s, ln2_g, ln2_b):
    bsz, seq, d = x.shape
    depth = w_ada.shape[0]
    n_exp = w_router.shape[2]
    n_tok = bsz * seq
    n_asg = n_tok * TOP_K
    alpha = (2.0 * depth) ** 0.25
    n_blocks = -(-(n_asg + n_exp * (MOE_BLOCK - 1)) // MOE_BLOCK)
    g_rows = min(GATHER_ROWS, n_asg)
    n_rows_padded = -(-(n_blocks * MOE_BLOCK) // g_rows) * g_rows

    for l in range(depth):
        mod = _ada(c, w_ada[l], b_ada[l]).reshape(bsz, 6, d)
        conv_p = jnp.stack([conv_b[l], conv_ln_g[l], conv_ln_b[l]])
        x1 = _mix(x, mod, w_in[l].astype(BF16), sinks[l], conv_w[l], conv_p, w_o[l].astype(BF16),
                  jnp.stack([ln1_g[l], ln1_b[l]]), alpha)
        x1 = x1.reshape(n_tok, d)
        h2, shared, eidx_t, ew_t = _ffn_pre(
            x1, mod, w_gate_s[l].astype(BF16), w_up_s[l].astype(BF16), w_down_s[l].astype(BF16),
            w_router[l].T.astype(BF16), router_bias[l].reshape(n_exp, 1), seq)
        row_tok, pos, block_exp, n_used = _dispatch_tables(eidx_t, n_exp, n_blocks, n_rows_padded)
        xs = _gather_rows(h2, row_tok)
        y = _experts(xs, block_exp, n_used, w_gate_e[l], w_up_e[l], w_down_e[l], n_blocks)
        yg = _gather_rows(y, pos).reshape(TOP_K, n_tok, d)
        x = _final(x1, shared, yg, ew_t.T, mod, jnp.stack([ln2_g[l], ln2_b[l]]), seq, alpha)
        x = x.reshape(bsz, seq, d)
    return x
```

```python
import functools
import math

import jax
import jax.numpy as jnp
from jax import lax
from jax.experimental import pallas as pl
from jax.experimental.pallas import tpu as pltpu

F32 = jnp.float32
BF16 = jnp.bfloat16
NEG_INF = float("-inf")

HEAD_DIM = 64
WINDOW = 128
CONV_KERNEL = 31
CONV_HIST = 32
TOP_K = 8
N_EXPERT_GROUPS = 8
TOPK_EXPERT_GROUPS = 4
ROUTED_SCALE = 2.5
LN_EPS = 1e-5

LANES = 128
SUBLANES = 8
SEQ_TILE = 512
TOK_TILE = 512
FIN_TILE = 512
SUPER_TILE = 4096
MOE_ROWS = 160
RMW_UNROLL = 4
CONV_ROWS = 64
VMEM_LIMIT = 56 * 1024 * 1024


def _alibi_slopes(n_heads):
    return [2.0 ** (-8.0 * (i + 1) / n_heads) for i in range(n_heads)]


def _layer_norm_rows(z, g, b):
    mu = jnp.mean(z, axis=-1, keepdims=True)
    d = z - mu
    var = jnp.mean(d * d, axis=-1, keepdims=True)
    return d * lax.rsqrt(var + LN_EPS) * g + b


def _to_slab(ref, val):
    n = val.shape[0]
    for s in range(SUBLANES):
        ref[pl.ds(s, n, stride=SUBLANES), :] = val[:, s * LANES:(s + 1) * LANES]


def _from_slab(ref, n):
    return jnp.concatenate([ref[pl.ds(s, n, stride=SUBLANES), :] for s in range(SUBLANES)], axis=1)


def _ada_kernel(c_ref, w_ref, b_ref, o_ref):
    c = c_ref[...]
    ca = (c * jax.nn.sigmoid(c)).astype(BF16)
    o_ref[...] = jnp.dot(ca, w_ref[...].astype(BF16), preferred_element_type=F32) + b_ref[...]


def _ada(c, w_ada, b_ada):
    bsz, d = c.shape
    n_out = w_ada.shape[1]
    return pl.pallas_call(
        _ada_kernel,
        grid=(n_out // d,),
        in_specs=[
            pl.BlockSpec((bsz, d), lambda j: (0, 0)),
            pl.BlockSpec((d, d), lambda j: (0, j)),
            pl.BlockSpec((1, d), lambda j: (0, j)),
        ],
        out_specs=pl.BlockSpec((bsz, d), lambda j: (0, j)),
        out_shape=jax.ShapeDtypeStruct((bsz, n_out), F32),
        compiler_params=pltpu.CompilerParams(vmem_limit_bytes=VMEM_LIMIT),
        name="ada",
    )(c, w_ada, b_ada.reshape(1, n_out))


def _mix_kernel(sinks_ref, x_ref, mod_ref, win_ref, convw_ref, convp_ref, wo_ref, ln_ref, o_ref,
                q_ref, ke_ref, ve_ref, glu_ref, cat_ref, *, ts, aw, alpha):
    s_idx = pl.program_id(1)
    n_heads = aw // HEAD_DIM
    slopes = _alibi_slopes(n_heads)
    cw = cat_ref.shape[1] - aw

    @pl.when(s_idx == 0)
    def _():
        ke_ref[:, 0:WINDOW, :] = jnp.zeros((4, WINDOW, LANES), BF16)
        ve_ref[:, 0:WINDOW, :] = jnp.zeros((4, WINDOW, LANES), BF16)
        glu_ref[0:CONV_HIST, :] = jnp.zeros((CONV_HIST, cw), F32)

    @pl.when(s_idx > 0)
    def _():
        ke_ref[:, 0:WINDOW, :] = ke_ref[:, ts:ts + WINDOW, :]
        ve_ref[:, 0:WINDOW, :] = ve_ref[:, ts:ts + WINDOW, :]
        glu_ref[0:CONV_HIST, :] = glu_ref[ts:ts + CONV_HIST, :]

    x = x_ref[...]
    h = (x * (1.0 + mod_ref[1:2, :]) + mod_ref[0:1, :]).astype(BF16)

    q = jnp.dot(h, win_ref[:, 0:aw], preferred_element_type=F32)
    q_ref[...] = (q * (1.0 / math.sqrt(HEAD_DIM))).astype(BF16)
    kv = jnp.dot(h, win_ref[:, aw:aw + 2 * LANES], preferred_element_type=F32)
    lo = lax.broadcasted_iota(jnp.int32, (ts, LANES), 1) < HEAD_DIM
    for dst, t in ((ke_ref, kv[:, 0:LANES]), (ve_ref, kv[:, LANES:2 * LANES])):
        t_r = pltpu.roll(t, HEAD_DIM, axis=1)
        dst[0, WINDOW:WINDOW + ts, :] = jnp.where(lo, t, 0.0).astype(BF16)
        dst[1, WINDOW:WINDOW + ts, :] = jnp.where(lo, 0.0, t_r).astype(BF16)
        dst[2, WINDOW:WINDOW + ts, :] = jnp.where(lo, t_r, 0.0).astype(BF16)
        dst[3, WINDOW:WINDOW + ts, :] = jnp.where(lo, 0.0, t).astype(BF16)
    u0 = aw + 2 * LANES
    ga = jnp.dot(h, win_ref[:, u0:u0 + cw], preferred_element_type=F32)
    gb = jnp.dot(h, win_ref[:, u0 + cw:u0 + 2 * cw], preferred_element_type=F32)
    glu_ref[CONV_HIST:CONV_HIST + ts, :] = ga * jax.nn.sigmoid(gb)

    qi = lax.broadcasted_iota(jnp.int32, (WINDOW, 2 * WINDOW), 0)
    kj = lax.broadcasted_iota(jnp.int32, (WINDOW, 2 * WINDOW), 1)
    dist = WINDOW + qi - kj
    band = (dist >= 0) & (dist < WINDOW)
    neg_dist = -dist.astype(F32)
    bias_any = jnp.where(band, neg_dist, NEG_INF)
    bias_first = jnp.where(band & ((kj >= WINDOW) | (s_idx > 0)), neg_dist, NEG_INF)
    for i in range(ts // WINDOW):
        r0 = i * WINDOW
        bias = bias_first if i == 0 else bias_any
        for pair in range(aw // LANES):
            g = (2 * pair) // (n_heads // 2)
            qp = q_ref[r0:r0 + WINDOW, pair * LANES:(pair + 1) * LANES]
            out_pair = None
            for par in range(2):
                hd = 2 * pair + par
                kk = ke_ref[2 * g + par, r0:r0 + 2 * WINDOW, :]
                s = lax.dot_general(qp, kk, (((1,), (1,)), ((), ())), preferred_element_type=F32)
                s = s + slopes[hd] * bias
                sink = sinks_ref[hd]
                m = jnp.maximum(jnp.max(s, axis=-1, keepdims=True), sink)
                p = jnp.exp(s - m)
                denom = jnp.sum(p, axis=-1, keepdims=True) + jnp.exp(sink - m)
                vv = ve_ref[2 * g + par, r0:r0 + 2 * WINDOW, :]
                o = jnp.dot(p.astype(BF16), vv, preferred_element_type=F32) * (1.0 / denom)
                out_pair = o if out_pair is None else out_pair + o
            cat_ref[r0:r0 + WINDOW, pair * LANES:(pair + 1) * LANES] = out_pair.astype(BF16)

    conv_b = convp_ref[0:1, :]
    cln_g = convp_ref[1:2, :]
    cln_b = convp_ref[2:3, :]
    off = CONV_HIST - (CONV_KERNEL - 1)
    for c in range(ts // CONV_ROWS):
        c0 = c * CONV_ROWS
        acc = jnp.broadcast_to(conv_b, (CONV_ROWS, cw))
        for j in range(CONV_KERNEL):
            acc = acc + glu_ref[c0 + off + j:c0 + off + j + CONV_ROWS, :] * convw_ref[j:j + 1, :]
        yn = _layer_norm_rows(acc, cln_g, cln_b)
        cat_ref[c0:c0 + CONV_ROWS, aw:aw + cw] = (yn * jax.nn.sigmoid(yn)).astype(BF16)

    mix = jnp.dot(cat_ref[...], wo_ref[...], preferred_element_type=F32)
    z = alpha * x + (1.0 + mod_ref[2:3, :]) * mix
    o_ref[...] = _layer_norm_rows(z, ln_ref[0:1, :], ln_ref[1:2, :])


def _mix(x, mod, w_in, sinks, conv_w, conv_p, w_o, ln, alpha):
    bsz, seq, d = x.shape
    cw = conv_w.shape[1]
    aw = d - cw
    ts = min(SEQ_TILE, seq)
    assert seq % ts == 0 and ts % WINDOW == 0 and aw % LANES == 0
    assert (aw // HEAD_DIM) // 4 == 2, "kernel packs exactly two KV heads into one lane group"
    assert w_in.shape[1] == aw + 2 * LANES + 2 * cw
    kern = functools.partial(_mix_kernel, ts=ts, aw=aw, alpha=alpha)
    const = lambda b, s: (0, 0)
    return pl.pallas_call(
        kern,
        grid=(bsz, seq // ts),
        in_specs=[
            pl.BlockSpec(memory_space=pltpu.SMEM),
            pl.BlockSpec((None, ts, d), lambda b, s: (b, s, 0)),
            pl.BlockSpec((None, 6, d), lambda b, s: (b, 0, 0)),
            pl.BlockSpec(w_in.shape, const),
            pl.BlockSpec(conv_w.shape, const),
            pl.BlockSpec(conv_p.shape, const),
            pl.BlockSpec(w_o.shape, const),
            pl.BlockSpec(ln.shape, const),
        ],
        out_specs=pl.BlockSpec((None, ts, d), lambda b, s: (b, s, 0)),
        out_shape=jax.ShapeDtypeStruct((bsz, seq, d), F32),
        scratch_shapes=[
            pltpu.VMEM((ts, aw), BF16),
            pltpu.VMEM((4, ts + WINDOW, LANES), BF16),
            pltpu.VMEM((4, ts + WINDOW, LANES), BF16),
            pltpu.VMEM((ts + CONV_HIST, cw), F32),
            pltpu.VMEM((ts, d), BF16),
        ],
        compiler_params=pltpu.CompilerParams(
            dimension_semantics=("arbitrary", "arbitrary"), vmem_limit_bytes=VMEM_LIMIT),
        name="mix",
    )(sinks, x, mod, w_in, conv_w, conv_p, w_o, ln)


def _ffn_pre_kernel(x_ref, mod_ref, wgs_ref, wus_ref, wds_ref, wrt_ref, rb_ref,
                    h_ref, sh_ref, ei_ref, ew_ref, cnt_ref):
    tm = x_ref.shape[0]
    n_exp = wrt_ref.shape[0]
    per = n_exp // N_EXPERT_GROUPS
    hf = x_ref[...] * (1.0 + mod_ref[4:5, :]) + mod_ref[3:4, :]
    _to_slab(h_ref, hf)
    h = hf.astype(BF16)

    gate = jnp.dot(h, wgs_ref[...], preferred_element_type=F32)
    up = jnp.dot(h, wus_ref[...], preferred_element_type=F32)
    act = (gate * jax.nn.sigmoid(gate) * up).astype(BF16)
    sh_ref[...] = jnp.dot(act, wds_ref[...], preferred_element_type=F32)

    logits = lax.dot_general(wrt_ref[...], h, (((1,), (1,)), ((), ())), preferred_element_type=F32)
    scores = jax.nn.sigmoid(logits)
    sel = scores + rb_ref[...]

    iota_p = lax.broadcasted_iota(jnp.int32, (per, tm), 0).astype(F32)
    gs_rows = []
    for g in range(N_EXPERT_GROUPS):
        blk = sel[g * per:(g + 1) * per, :]
        m1 = jnp.max(blk, axis=0, keepdims=True)
        i1 = jnp.min(jnp.where(blk == m1, iota_p, float(per)), axis=0, keepdims=True)
        m2 = jnp.max(jnp.where(iota_p == i1, NEG_INF, blk), axis=0, keepdims=True)
        gs_rows.append(m1 + m2)
    gs = jnp.concatenate(gs_rows, axis=0)
    iota_g = lax.broadcasted_iota(jnp.int32, (N_EXPERT_GROUPS, tm), 0).astype(F32)
    gmask = jnp.zeros((N_EXPERT_GROUPS, tm), jnp.bool_)
    for _ in range(TOPK_EXPERT_GROUPS):
        m = jnp.max(gs, axis=0, keepdims=True)
        gi = jnp.min(jnp.where(gs == m, iota_g, float(N_EXPERT_GROUPS)), axis=0, keepdims=True)
        hit = iota_g == gi
        gmask = gmask | hit
        gs = jnp.where(hit, NEG_INF, gs)
    emask = jnp.concatenate(
        [jnp.broadcast_to(gmask[g:g + 1, :], (per, tm)) for g in range(N_EXPERT_GROUPS)], axis=0)
    cand = jnp.where(emask, sel, NEG_INF)

    iota_e = lax.broadcasted_iota(jnp.int32, (n_exp, tm), 0).astype(F32)
    idx_rows, w_rows = [], []
    for _ in range(TOP_K):
        m = jnp.max(cand, axis=0, keepdims=True)
        ei = jnp.min(jnp.where(cand == m, iota_e, float(n_exp)), axis=0, keepdims=True)
        hit = iota_e == ei
        w_rows.append(jnp.sum(jnp.where(hit, scores, 0.0), axis=0, keepdims=True))
        idx_rows.append(ei)
        cand = jnp.where(hit, NEG_INF, cand)
    wk = jnp.concatenate(w_rows, axis=0)
    ew_ref[...] = wk / jnp.sum(wk, axis=0, keepdims=True) * ROUTED_SCALE
    ei_ref[...] = jnp.concatenate(idx_rows, axis=0).astype(jnp.int32)
    chosen = jnp.where(emask & (cand == NEG_INF), 1.0, 0.0).astype(BF16)
    cnt_ref[...] = jnp.dot(chosen, jnp.ones((tm, LANES), BF16), preferred_element_type=F32)


def _ffn_pre(x1, mod, wgs, wus, wds, wrt, rbias, seq):
    n_tok, d = x1.shape
    tm = min(TOK_TILE, seq)
    assert seq % tm == 0 and n_tok % tm == 0 and d == SUBLANES * LANES
    n_exp = wrt.shape[0]
    const = lambda i: (0, 0)
    return pl.pallas_call(
        _ffn_pre_kernel,
        grid=(n_tok // tm,),
        in_specs=[
            pl.BlockSpec((tm, d), lambda i: (i, 0)),
            pl.BlockSpec((None, 6, d), lambda i: ((i * tm) // seq, 0, 0)),
            pl.BlockSpec(wgs.shape, const),
            pl.BlockSpec(wus.shape, const),
            pl.BlockSpec(wds.shape, const),
            pl.BlockSpec(wrt.shape, const),
            pl.BlockSpec(rbias.shape, const),
        ],
        out_specs=[
            pl.BlockSpec((tm * SUBLANES, LANES), lambda i: (i, 0)),
            pl.BlockSpec((tm, d), lambda i: (i, 0)),
            pl.BlockSpec((TOP_K, tm), lambda i: (0, i)),
            pl.BlockSpec((TOP_K, tm), lambda i: (0, i)),
            pl.BlockSpec((None, n_exp, LANES), lambda i: (i, 0, 0)),
        ],
        out_shape=[
            jax.ShapeDtypeStruct((n_tok * SUBLANES, LANES), F32),
            jax.ShapeDtypeStruct((n_tok, d), F32),
            jax.ShapeDtypeStruct((TOP_K, n_tok), jnp.int32),
            jax.ShapeDtypeStruct((TOP_K, n_tok), F32),
            jax.ShapeDtypeStruct((n_tok // tm, n_exp, LANES), F32),
        ],
        compiler_params=pltpu.CompilerParams(
            dimension_semantics=("arbitrary",), vmem_limit_bytes=VMEM_LIMIT),
        name="ffn_pre",
    )(x1, mod, wgs, wus, wds, wrt, rbias)


def _moe_kernel(off_ref, cnt_ref, tok_ref, w_ref, h_hbm, wg_ref, wu_ref, wd_ref, out_hbm,
                h_vmem, acc_ref, xst_ref, yst_ref, sem, *, tt):
    j = pl.program_id(0)
    e = pl.program_id(1)
    n_exp = pl.num_programs(1)
    n_asg = tok_ref.shape[2]
    rows = xst_ref.shape[0] // SUBLANES

    @pl.when(e == 0)
    def _():
        load = pltpu.make_async_copy(h_hbm.at[pl.ds(j * (tt * SUBLANES), tt * SUBLANES)], h_vmem, sem)
        load.start()
        acc_ref[...] = jnp.zeros_like(acc_ref)
        load.wait()

    off = off_ref[j * n_exp + e]
    cnt = cnt_ref[j * n_exp + e]

    def chunk(c, carry):
        base = off + c * rows
        n_valid = jnp.minimum(rows, cnt - c * rows)

        def gather(r, carry):
            t = tok_ref[0, 0, jnp.minimum(base + r, n_asg - 1)]
            xst_ref[pl.ds(pl.multiple_of(r * SUBLANES, SUBLANES), SUBLANES), :] = (
                h_vmem[pl.ds(pl.multiple_of(t * SUBLANES, SUBLANES), SUBLANES), :])
            return carry

        lax.fori_loop(0, rows, gather, 0, unroll=8)

        x = _from_slab(xst_ref, rows).astype(BF16)
        gate = jnp.dot(x, wg_ref[...], preferred_element_type=F32)
        up = jnp.dot(x, wu_ref[...], preferred_element_type=F32)
        act = (gate * jax.nn.sigmoid(gate) * up).astype(BF16)
        _to_slab(yst_ref, jnp.dot(act, wd_ref[...], preferred_element_type=F32))

        def scatter(g, carry):
            updates = []
            for i in range(RMW_UNROLL):
                r = g * RMW_UNROLL + i
                a = jnp.minimum(base + r, n_asg - 1)
                valid = r < n_valid
                t = jnp.where(valid, tok_ref[0, 0, a], tt)
                w = jnp.where(valid, w_ref[0, 0, a], 0.0)
                dst = pl.ds(pl.multiple_of(t * SUBLANES, SUBLANES), SUBLANES)
                src = pl.ds(pl.multiple_of(r * SUBLANES, SUBLANES), SUBLANES)
                updates.append((dst, acc_ref[dst, :] + w * yst_ref[src, :]))
            for dst, val in updates:
                acc_ref[dst, :] = val
            return carry

        lax.fori_loop(0, (n_valid + RMW_UNROLL - 1) // RMW_UNROLL, scatter, 0)
        return carry

    lax.fori_loop(0, (cnt + rows - 1) // rows, chunk, 0)

    @pl.when(e == n_exp - 1)
    def _():
        store = pltpu.make_async_copy(
            acc_ref.at[pl.ds(0, tt * SUBLANES)], out_hbm.at[pl.ds(j * (tt * SUBLANES), tt * SUBLANES)], sem)
        store.start()
        store.wait()


def _moe(h_slab, tok_s, w_s, off, cnt, w_gate, w_up, w_down, tt):
    n_tok = h_slab.shape[0] // SUBLANES
    n_exp, d, f = w_gate.shape
    n_super = n_tok // tt
    rows = min(MOE_ROWS, tt * TOP_K)
    assert rows % (2 * SUBLANES) == 0 and rows % RMW_UNROLL == 0
    w_map = lambda j, e, off, cnt: (e, 0, 0)
    lst_map = lambda j, e, off, cnt: (j, 0, 0)
    return pl.pallas_call(
        functools.partial(_moe_kernel, tt=tt),
        grid_spec=pltpu.PrefetchScalarGridSpec(
            num_scalar_prefetch=2,
            grid=(n_super, n_exp),
            in_specs=[
                pl.BlockSpec((1, 1, tt * TOP_K), lst_map, memory_space=pltpu.SMEM),
                pl.BlockSpec((1, 1, tt * TOP_K), lst_map, memory_space=pltpu.SMEM),
                pl.BlockSpec(memory_space=pl.ANY),
                pl.BlockSpec((None, d, f), w_map),
                pl.BlockSpec((None, d, f), w_map),
                pl.BlockSpec((None, f, d), w_map),
            ],
            out_specs=pl.BlockSpec(memory_space=pl.ANY),
            scratch_shapes=[
                pltpu.VMEM((tt * SUBLANES, LANES), F32),
                pltpu.VMEM(((tt + 1) * SUBLANES, LANES), F32),
                pltpu.VMEM((rows * SUBLANES, LANES), F32),
                pltpu.VMEM((rows * SUBLANES, LANES), F32),
                pltpu.SemaphoreType.DMA,
            ],
        ),
        out_shape=jax.ShapeDtypeStruct((n_tok * SUBLANES, LANES), F32),
        compiler_params=pltpu.CompilerParams(
            dimension_semantics=("arbitrary", "arbitrary"), vmem_limit_bytes=VMEM_LIMIT),
        name="moe",
    )(off, cnt, tok_s, w_s, h_slab, w_gate, w_up, w_down)


def _dispatch_lists(eidx_t, ew_t, counts, tt):
    n_tok = eidx_t.shape[1]
    n_tiles, n_exp = counts.shape[0], counts.shape[1]
    n_super = n_tok // tt
    t = jnp.arange(n_tok, dtype=jnp.int32)
    key = ((t // tt) * n_exp + eidx_t) * tt + t % tt
    key_s, w_s = lax.sort((key.reshape(-1), ew_t.reshape(-1)), num_keys=1)
    tok_s = key_s % tt
    cnt = counts[:, :, 0].astype(jnp.int32).reshape(n_super, n_tiles // n_super, n_exp).sum(axis=1)
    off = jnp.cumsum(cnt, axis=1) - cnt
    shape = (n_super, 1, tt * TOP_K)
    return tok_s.reshape(shape), w_s.reshape(shape), off.reshape(-1), cnt.reshape(-1)


def _final_kernel(x_ref, sh_ref, ffn_ref, mod_ref, ln_ref, o_ref, *, alpha):
    tm = x_ref.shape[0]
    ffn = sh_ref[...] + _from_slab(ffn_ref, tm)
    z = alpha * x_ref[...] + (1.0 + mod_ref[5:6, :]) * ffn
    o_ref[...] = _layer_norm_rows(z, ln_ref[0:1, :], ln_ref[1:2, :])


def _final(x1, shared, routed_slab, mod, ln, seq, alpha):
    n_tok, d = x1.shape
    tm = min(FIN_TILE, seq)
    assert seq % tm == 0
    return pl.pallas_call(
        functools.partial(_final_kernel, alpha=alpha),
        grid=(n_tok // tm,),
        in_specs=[
            pl.BlockSpec((tm, d), lambda i: (i, 0)),
            pl.BlockSpec((tm, d), lambda i: (i, 0)),
            pl.BlockSpec((tm * SUBLANES, LANES), lambda i: (i, 0)),
            pl.BlockSpec((None, 6, d), lambda i: ((i * tm) // seq, 0, 0)),
            pl.BlockSpec(ln.shape, lambda i: (0, 0)),
        ],
        out_specs=pl.BlockSpec((tm, d), lambda i: (i, 0)),
        out_shape=jax.ShapeDtypeStruct((n_tok, d), F32),
        compiler_params=pltpu.CompilerParams(
            dimension_semantics=("arbitrary",), vmem_limit_bytes=VMEM_LIMIT),
        name="final",
    )(x1, shared, routed_slab, mod, ln)


def kernel(x, c, w_ada, b_ada, w_in, sinks, conv_w, conv_b, conv_ln_g, conv_ln_b, w_o, ln1_g, ln1_b,
           w_router, router_bias, w_gate_e, w_up_e, w_down_e, w_gate_s, w_up_s, w_down_s, ln2_g, ln2_b):
    bsz, seq, d = x.shape
    depth = w_ada.shape[0]
    n_exp = w_router.shape[2]
    n_tok = bsz * seq
    alpha = (2.0 * depth) ** 0.25
    tt = min(SUPER_TILE, n_tok)
    assert n_tok % tt == 0 and tt % min(TOK_TILE, seq) == 0

    for l in range(depth):
        mod = _ada(c, w_ada[l], b_ada[l]).reshape(bsz, 6, d)
        conv_p = jnp.stack([conv_b[l], conv_ln_g[l], conv_ln_b[l]])
        x1 = _mix(x, mod, w_in[l].astype(BF16), sinks[l], conv_w[l], conv_p, w_o[l].astype(BF16),
                  jnp.stack([ln1_g[l], ln1_b[l]]), alpha)
        x1 = x1.reshape(n_tok, d)
        h_slab, shared, eidx_t, ew_t, counts = _ffn_pre(
            x1, mod, w_gate_s[l].astype(BF16), w_up_s[l].astype(BF16), w_down_s[l].astype(BF16),
            w_router[l].T.astype(BF16), router_bias[l].reshape(n_exp, 1), seq)
        tok_s, w_s, off, cnt = _dispatch_lists(eidx_t, ew_t, counts, tt)
        routed = _moe(h_slab, tok_s, w_s, off, cnt, w_gate_e[l].astype(BF16), w_up_e[l].astype(BF16),
                      w_down_e[l].astype(BF16), tt)
        x = _final(x1, shared, routed, mod, jnp.stack([ln2_g[l], ln2_b[l]]), seq, alpha)
        x = x.reshape(bsz, seq, d)
    return x
```

```python
import functools
import math

import jax
import jax.numpy as jnp
from jax import lax
from jax.experimental import pallas as pl
from jax.experimental.pallas import tpu as pltpu

F32 = jnp.float32
BF16 = jnp.bfloat16
NEG_INF = float("-inf")

HEAD_DIM = 64
WINDOW = 128
CONV_KERNEL = 31
CONV_HIST = 32
TOP_K = 8
N_EXPERT_GROUPS = 8
TOPK_EXPERT_GROUPS = 4
ROUTED_SCALE = 2.5
LN_EPS = 1e-5

LANES = 128
SUBLANES = 8
SEQ_TILE = 512
TOK_TILE = 512
FIN_TILE = 512
SUPER_TILE = 4096
MOE_ROWS = 160
RMW_UNROLL = 4
CONV_ROWS = 64
VMEM_LIMIT = 56 * 1024 * 1024


def _alibi_slopes(n_heads):
    return [2.0 ** (-8.0 * (i + 1) / n_heads) for i in range(n_heads)]


def _layer_norm_rows(z, g, b):
    mu = jnp.mean(z, axis=-1, keepdims=True)
    d = z - mu
    var = jnp.mean(d * d, axis=-1, keepdims=True)
    return d * lax.rsqrt(var + LN_EPS) * g + b


def _to_slab(ref, val):
    n = val.shape[0]
    for s in range(SUBLANES):
        ref[pl.ds(s, n, stride=SUBLANES), :] = val[:, s * LANES:(s + 1) * LANES]


def _from_slab(ref, n):
    return jnp.concatenate([ref[pl.ds(s, n, stride=SUBLANES), :] for s in range(SUBLANES)], axis=1)


def _ada_kernel(c_ref, w_ref, b_ref, o_ref):
    c = c_ref[...]
    ca = (c * jax.nn.sigmoid(c)).astype(BF16)
    o_ref[...] = jnp.dot(ca, w_ref[...].astype(BF16), preferred_element_type=F32) + b_ref[...]


def _ada(c, w_ada, b_ada):
    bsz, d = c.shape
    n_out = w_ada.shape[1]
    return pl.pallas_call(
        _ada_kernel,
        grid=(n_out // d,),
        in_specs=[
            pl.BlockSpec((bsz, d), lambda j: (0, 0)),
            pl.BlockSpec((d, d), lambda j: (0, j)),
            pl.BlockSpec((1, d), lambda j: (0, j)),
        ],
        out_specs=pl.BlockSpec((bsz, d), lambda j: (0, j)),
        out_shape=jax.ShapeDtypeStruct((bsz, n_out), F32),
        compiler_params=pltpu.CompilerParams(vmem_limit_bytes=VMEM_LIMIT),
        name="ada",
    )(c, w_ada, b_ada.reshape(1, n_out))


def _mix_kernel(sinks_ref, x_ref, mod_ref, win_ref, convw_ref, convp_ref, wo_ref, ln_ref, o_ref,
                q_ref, ke_ref, ve_ref, glu_ref, gsh_ref, cat_ref, *, ts, aw, alpha):
    s_idx = pl.program_id(1)
    n_heads = aw // HEAD_DIM
    slopes = _alibi_slopes(n_heads)
    cw = cat_ref.shape[1] - aw

    @pl.when(s_idx == 0)
    def _():
        ke_ref[:, 0:WINDOW, :] = jnp.zeros((4, WINDOW, LANES), BF16)
        ve_ref[:, 0:WINDOW, :] = jnp.zeros((4, WINDOW, LANES), BF16)
        glu_ref[0:CONV_HIST, :] = jnp.zeros((CONV_HIST, cw), F32)

    @pl.when(s_idx > 0)
    def _():
        ke_ref[:, 0:WINDOW, :] = ke_ref[:, ts:ts + WINDOW, :]
        ve_ref[:, 0:WINDOW, :] = ve_ref[:, ts:ts + WINDOW, :]
        glu_ref[0:CONV_HIST, :] = glu_ref[ts:ts + CONV_HIST, :]

    x = x_ref[...]
    h = (x * (1.0 + mod_ref[1:2, :]) + mod_ref[0:1, :]).astype(BF16)

    q = jnp.dot(h, win_ref[:, 0:aw], preferred_element_type=F32)
    q_ref[...] = (q * (1.0 / math.sqrt(HEAD_DIM))).astype(BF16)
    kv = jnp.dot(h, win_ref[:, aw:aw + 2 * LANES], preferred_element_type=F32)
    lo = lax.broadcasted_iota(jnp.int32, (ts, LANES), 1) < HEAD_DIM
    for dst, t in ((ke_ref, kv[:, 0:LANES]), (ve_ref, kv[:, LANES:2 * LANES])):
        t_r = pltpu.roll(t, HEAD_DIM, axis=1)
        dst[0, WINDOW:WINDOW + ts, :] = jnp.where(lo, t, 0.0).astype(BF16)
        dst[1, WINDOW:WINDOW + ts, :] = jnp.where(lo, 0.0, t_r).astype(BF16)
        dst[2, WINDOW:WINDOW + ts, :] = jnp.where(lo, t_r, 0.0).astype(BF16)
        dst[3, WINDOW:WINDOW + ts, :] = jnp.where(lo, 0.0, t).astype(BF16)
    u0 = aw + 2 * LANES
    ga = jnp.dot(h, win_ref[:, u0:u0 + cw], preferred_element_type=F32)
    gb = jnp.dot(h, win_ref[:, u0 + cw:u0 + 2 * cw], preferred_element_type=F32)
    glu_ref[CONV_HIST:CONV_HIST + ts, :] = ga * jax.nn.sigmoid(gb)

    qi = lax.broadcasted_iota(jnp.int32, (WINDOW, 2 * WINDOW), 0)
    kj = lax.broadcasted_iota(jnp.int32, (WINDOW, 2 * WINDOW), 1)
    dist = WINDOW + qi - kj
    band = (dist >= 0) & (dist < WINDOW)
    neg_dist = -dist.astype(F32)
    bias_any = jnp.where(band, neg_dist, NEG_INF)
    bias_first = jnp.where(band & ((kj >= WINDOW) | (s_idx > 0)), neg_dist, NEG_INF)
    for i in range(ts // WINDOW):
        r0 = i * WINDOW
        bias = bias_first if i == 0 else bias_any
        for pair in range(aw // LANES):
            g = (2 * pair) // (n_heads // 2)
            qp = q_ref[r0:r0 + WINDOW, pair * LANES:(pair + 1) * LANES]
            out_pair = None
            for par in range(2):
                hd = 2 * pair + par
                kk = ke_ref[2 * g + par, r0:r0 + 2 * WINDOW, :]
                s = lax.dot_general(qp, kk, (((1,), (1,)), ((), ())), preferred_element_type=F32)
                s = s + slopes[hd] * bias
                sink = sinks_ref[hd]
                m = jnp.maximum(jnp.max(s, axis=-1, keepdims=True), sink)
                p = jnp.exp(s - m)
                denom = jnp.sum(p, axis=-1, keepdims=True) + jnp.exp(sink - m)
                vv = ve_ref[2 * g + par, r0:r0 + 2 * WINDOW, :]
                o = jnp.dot(p.astype(BF16), vv, preferred_element_type=F32) * (1.0 / denom)
                out_pair = o if out_pair is None else out_pair + o
            cat_ref[r0:r0 + WINDOW, pair * LANES:(pair + 1) * LANES] = out_pair.astype(BF16)

    conv_b = convp_ref[0:1, :]
    cln_g = convp_ref[1:2, :]
    cln_b = convp_ref[2:3, :]
    off = CONV_HIST - (CONV_KERNEL - 1)
    n_sh = gsh_ref.shape[1]
    for p in range(1, SUBLANES):
        gsh_ref[p - 1] = glu_ref[p:p + n_sh, :]
    for c in range(ts // CONV_ROWS):
        c0 = c * CONV_ROWS
        acc = jnp.broadcast_to(conv_b, (CONV_ROWS, cw))
        for j in range(CONV_KERNEL):
            a, p = divmod(off + j, SUBLANES)
            r0 = c0 + a * SUBLANES
            tap = glu_ref[r0:r0 + CONV_ROWS, :] if p == 0 else gsh_ref[p - 1, r0:r0 + CONV_ROWS, :]
            acc = acc + tap * convw_ref[j:j + 1, :]
        yn = _layer_norm_rows(acc, cln_g, cln_b)
        cat_ref[c0:c0 + CONV_ROWS, aw:aw + cw] = (yn * jax.nn.sigmoid(yn)).astype(BF16)

    mix = jnp.dot(cat_ref[...], wo_ref[...], preferred_element_type=F32)
    z = alpha * x + (1.0 + mod_ref[2:3, :]) * mix
    o_ref[...] = _layer_norm_rows(z, ln_ref[0:1, :], ln_ref[1:2, :])


def _mix(x, mod, w_in, sinks, conv_w, conv_p, w_o, ln, alpha):
    bsz, seq, d = x.shape
    cw = conv_w.shape[1]
    aw = d - cw
    ts = min(SEQ_TILE, seq)
    assert seq % ts == 0 and ts % WINDOW == 0 and aw % LANES == 0
    assert (aw // HEAD_DIM) // 4 == 2, "kernel packs exactly two KV heads into one lane group"
    assert w_in.shape[1] == aw + 2 * LANES + 2 * cw
    kern = functools.partial(_mix_kernel, ts=ts, aw=aw, alpha=alpha)
    const = lambda b, s: (0, 0)
    return pl.pallas_call(
        kern,
        grid=(bsz, seq // ts),
        in_specs=[
            pl.BlockSpec(memory_space=pltpu.SMEM),
            pl.BlockSpec((None, ts, d), lambda b, s: (b, s, 0)),
            pl.BlockSpec((None, 6, d), lambda b, s: (b, 0, 0)),
            pl.BlockSpec(w_in.shape, const),
            pl.BlockSpec(conv_w.shape, const),
            pl.BlockSpec(conv_p.shape, const),
            pl.BlockSpec(w_o.shape, const),
            pl.BlockSpec(ln.shape, const),
        ],
        out_specs=pl.BlockSpec((None, ts, d), lambda b, s: (b, s, 0)),
        out_shape=jax.ShapeDtypeStruct((bsz, seq, d), F32),
        scratch_shapes=[
            pltpu.VMEM((ts, aw), BF16),
            pltpu.VMEM((4, ts + WINDOW, LANES), BF16),
            pltpu.VMEM((4, ts + WINDOW, LANES), BF16),
            pltpu.VMEM((ts + CONV_HIST, cw), F32),
            pltpu.VMEM((SUBLANES - 1, ts + CONV_HIST - SUBLANES, cw), F32),
            pltpu.VMEM((ts, d), BF16),
        ],
        compiler_params=pltpu.CompilerParams(
            dimension_semantics=("arbitrary", "arbitrary"), vmem_limit_bytes=VMEM_LIMIT),
        name="mix",
    )(sinks, x, mod, w_in, conv_w, conv_p, w_o, ln)


def _ffn_pre_kernel(x_ref, mod_ref, wgs_ref, wus_ref, wds_ref, wrt_ref, rb_ref,
                    h_ref, sh_ref, ei_ref, ew_ref, cnt_ref):
    tm = x_ref.shape[0]
    n_exp = wrt_ref.shape[0]
    per = n_exp // N_EXPERT_GROUPS
    hf = x_ref[...] * (1.0 + mod_ref[4:5, :]) + mod_ref[3:4, :]
    _to_slab(h_ref, hf)
    h = hf.astype(BF16)

    gate = jnp.dot(h, wgs_ref[...], preferred_element_type=F32)
    up = jnp.dot(h, wus_ref[...], preferred_element_type=F32)
    act = (gate * jax.nn.sigmoid(gate) * up).astype(BF16)
    sh_ref[...] = jnp.dot(act, wds_ref[...], preferred_element_type=F32)

    logits = lax.dot_general(wrt_ref[...], h, (((1,), (1,)), ((), ())), preferred_element_type=F32)
    scores = jax.nn.sigmoid(logits)
    sel = scores + rb_ref[...]

    iota_p = lax.broadcasted_iota(jnp.int32, (per, tm), 0).astype(F32)
    gs_rows = []
    for g in range(N_EXPERT_GROUPS):
        blk = sel[g * per:(g + 1) * per, :]
        m1 = jnp.max(blk, axis=0, keepdims=True)
        i1 = jnp.min(jnp.where(blk == m1, iota_p, float(per)), axis=0, keepdims=True)
        m2 = jnp.max(jnp.where(iota_p == i1, NEG_INF, blk), axis=0, keepdims=True)
        gs_rows.append(m1 + m2)
    gs = jnp.concatenate(gs_rows, axis=0)
    iota_g = lax.broadcasted_iota(jnp.int32, (N_EXPERT_GROUPS, tm), 0).astype(F32)
    gmask = jnp.zeros((N_EXPERT_GROUPS, tm), jnp.bool_)
    for _ in range(TOPK_EXPERT_GROUPS):
        m = jnp.max(gs, axis=0, keepdims=True)
        gi = jnp.min(jnp.where(gs == m, iota_g, float(N_EXPERT_GROUPS)), axis=0, keepdims=True)
        hit = iota_g == gi
        gmask = gmask | hit
        gs = jnp.where(hit, NEG_INF, gs)
    emask = jnp.concatenate(
        [jnp.broadcast_to(gmask[g:g + 1, :], (per, tm)) for g in range(N_EXPERT_GROUPS)], axis=0)
    cand = jnp.where(emask, sel, NEG_INF)

    iota_e = lax.broadcasted_iota(jnp.int32, (n_exp, tm), 0).astype(F32)
    idx_rows, w_rows = [], []
    for _ in range(TOP_K):
        m = jnp.max(cand, axis=0, keepdims=True)
        ei = jnp.min(jnp.where(cand == m, iota_e, float(n_exp)), axis=0, keepdims=True)
        hit = iota_e == ei
        w_rows.append(jnp.sum(jnp.where(hit, scores, 0.0), axis=0, keepdims=True))
        idx_rows.append(ei)
        cand = jnp.where(hit, NEG_INF, cand)
    wk = jnp.concatenate(w_rows, axis=0)
    ew_ref[...] = wk / jnp.sum(wk, axis=0, keepdims=True) * ROUTED_SCALE
    ei_ref[...] = jnp.concatenate(idx_rows, axis=0).astype(jnp.int32)
    chosen = jnp.where(emask & (cand == NEG_INF), 1.0, 0.0).astype(BF16)
    cnt_ref[...] = jnp.dot(chosen, jnp.ones((tm, LANES), BF16), preferred_element_type=F32)


def _ffn_pre(x1, mod, wgs, wus, wds, wrt, rbias, seq):
    n_tok, d = x1.shape
    tm = min(TOK_TILE, seq)
    assert seq % tm == 0 and n_tok % tm == 0 and d == SUBLANES * LANES
    n_exp = wrt.shape[0]
    const = lambda i: (0, 0)
    return pl.pallas_call(
        _ffn_pre_kernel,
        grid=(n_tok // tm,),
        in_specs=[
            pl.BlockSpec((tm, d), lambda i: (i, 0)),
            pl.BlockSpec((None, 6, d), lambda i: ((i * tm) // seq, 0, 0)),
            pl.BlockSpec(wgs.shape, const),
            pl.BlockSpec(wus.shape, const),
            pl.BlockSpec(wds.shape, const),
            pl.BlockSpec(wrt.shape, const),
            pl.BlockSpec(rbias.shape, const),
        ],
        out_specs=[
            pl.BlockSpec((tm * SUBLANES, LANES), lambda i: (i, 0)),
            pl.BlockSpec((tm, d), lambda i: (i, 0)),
            pl.BlockSpec((TOP_K, tm), lambda i: (0, i)),
            pl.BlockSpec((TOP_K, tm), lambda i: (0, i)),
            pl.BlockSpec((None, n_exp, LANES), lambda i: (i, 0, 0)),
        ],
        out_shape=[
            jax.ShapeDtypeStruct((n_tok * SUBLANES, LANES), F32),
            jax.ShapeDtypeStruct((n_tok, d), F32),
            jax.ShapeDtypeStruct((TOP_K, n_tok), jnp.int32),
            jax.ShapeDtypeStruct((TOP_K, n_tok), F32),
            jax.ShapeDtypeStruct((n_tok // tm, n_exp, LANES), F32),
        ],
        compiler_params=pltpu.CompilerParams(
            dimension_semantics=("arbitrary",), vmem_limit_bytes=VMEM_LIMIT),
        name="ffn_pre",
    )(x1, mod, wgs, wus, wds, wrt, rbias)


def _moe_kernel(off_ref, cnt_ref, tok_ref, w_ref, h_hbm, wg_ref, wu_ref, wd_ref, out_hbm,
                h_vmem, acc_ref, xa_ref, xb_ref, xd_ref, ya_ref, yb_ref, yd_ref, sem, *, tt):
    j = pl.program_id(0)
    e = pl.program_id(1)
    n_exp = pl.num_programs(1)
    n_asg = tok_ref.shape[2]
    rows = xa_ref.shape[0] // SUBLANES
    dummy_row = tt * SUBLANES
    last = pl.num_programs(0) * n_exp - 1

    def slab(r):
        if isinstance(r, int):
            return pl.ds(r * SUBLANES, SUBLANES)
        return pl.ds(pl.multiple_of(r * SUBLANES, SUBLANES), SUBLANES)

    def window(seg_off, seg_cnt):
        base = jnp.minimum(seg_off, n_asg - rows)
        lo = seg_off - base
        return base, lo, lo + jnp.minimum(rows, jnp.maximum(seg_cnt, 0))

    def gather_row(x_ref, base, r):
        x_ref[slab(r), :] = h_vmem[pl.ds(pl.multiple_of(tok_ref[0, 0, base + r], SUBLANES), SUBLANES), :]

    def scatter_group(y_ref, base, lo, hi, r0):
        updates = []
        for i in range(RMW_UNROLL):
            r = r0 + i
            valid = (r >= lo) & (r < hi)
            dst = pl.ds(pl.multiple_of(jnp.where(valid, tok_ref[0, 0, base + r], dummy_row), SUBLANES),
                        SUBLANES)
            w = jnp.where(valid, w_ref[0, 0, base + r], 0.0)
            updates.append((dst, acc_ref[dst, :] + w * y_ref[slab(r), :]))
        for dst, val in updates:
            acc_ref[dst, :] = val

    def expert_mlp(x_ref, y_ref):
        x = _from_slab(x_ref, rows).astype(BF16)
        gate = jnp.dot(x, wg_ref[...], preferred_element_type=F32)
        up = jnp.dot(x, wu_ref[...], preferred_element_type=F32)
        act = (gate * jax.nn.sigmoid(gate) * up).astype(BF16)
        _to_slab(y_ref, jnp.dot(act, wd_ref[...], preferred_element_type=F32))

    def gather_loop(x_ref, base):
        def body(r, carry):
            gather_row(x_ref, base, r)
            return carry
        lax.fori_loop(0, rows, body, 0, unroll=8)

    def scatter_loop(y_ref, base, lo, hi):
        def body(g, carry):
            scatter_group(y_ref, base, lo, hi, g * RMW_UNROLL)
            return carry
        lax.fori_loop(0, rows // RMW_UNROLL, body, 0)

    idx = j * n_exp + e
    off = off_ref[idx]
    cnt = cnt_ref[idx]
    base, lo, hi = window(off, cnt)

    @pl.when(e == 0)
    def _():
        load = pltpu.make_async_copy(h_hbm.at[pl.ds(j * (tt * SUBLANES), tt * SUBLANES)], h_vmem, sem)
        load.start()
        acc_ref[...] = jnp.zeros_like(acc_ref)
        ya_ref[...] = jnp.zeros_like(ya_ref)
        yb_ref[...] = jnp.zeros_like(yb_ref)
        load.wait()
        gather_loop(xa_ref, base)

    idx_p = jnp.maximum(idx - 1, 0)
    base_p, lo_p, hi_p = window(off_ref[idx_p], jnp.where(e > 0, cnt_ref[idx_p], 0))
    idx_n = jnp.minimum(idx + 1, last)
    base_n, _, _ = window(off_ref[idx_n], 0)

    def step(x_cur, x_next, y_cur, y_prev):
        for g in range(rows // RMW_UNROLL):
            scatter_group(y_prev, base_p, lo_p, hi_p, g * RMW_UNROLL)
        expert_mlp(x_cur, y_cur)
        for r in range(rows):
            gather_row(x_next, base_n, r)

    @pl.when(e % 2 == 0)
    def _():
        step(xa_ref, xb_ref, ya_ref, yb_ref)

    @pl.when(e % 2 == 1)
    def _():
        step(xb_ref, xa_ref, yb_ref, ya_ref)

    def extra(c, carry):
        base_c, lo_c, hi_c = window(off + c * rows, cnt - c * rows)
        gather_loop(xd_ref, base_c)
        expert_mlp(xd_ref, yd_ref)
        scatter_loop(yd_ref, base_c, lo_c, hi_c)
        return carry

    lax.fori_loop(1, (cnt + rows - 1) // rows, extra, 0)

    @pl.when(e == n_exp - 1)
    def _():
        @pl.when(e % 2 == 0)
        def _():
            scatter_loop(ya_ref, base, lo, hi)

        @pl.when(e % 2 == 1)
        def _():
            scatter_loop(yb_ref, base, lo, hi)

        store = pltpu.make_async_copy(
            acc_ref.at[pl.ds(0, tt * SUBLANES)], out_hbm.at[pl.ds(j * (tt * SUBLANES), tt * SUBLANES)], sem)
        store.start()
        store.wait()


def _moe(h_slab, tok_s, w_s, off, cnt, w_gate, w_up, w_down, tt):
    n_tok = h_slab.shape[0] // SUBLANES
    n_exp, d, f = w_gate.shape
    n_super = n_tok // tt
    rows = min(MOE_ROWS, tt * TOP_K)
    assert rows % (2 * SUBLANES) == 0 and rows % RMW_UNROLL == 0
    w_map = lambda j, e, off, cnt: (e, 0, 0)
    lst_map = lambda j, e, off, cnt: (j, 0, 0)
    return pl.pallas_call(
        functools.partial(_moe_kernel, tt=tt),
        grid_spec=pltpu.PrefetchScalarGridSpec(
            num_scalar_prefetch=2,
            grid=(n_super, n_exp),
            in_specs=[
                pl.BlockSpec((1, 1, tt * TOP_K), lst_map, memory_space=pltpu.SMEM),
                pl.BlockSpec((1, 1, tt * TOP_K), lst_map, memory_space=pltpu.SMEM),
                pl.BlockSpec(memory_space=pl.ANY),
                pl.BlockSpec((None, d, f), w_map),
                pl.BlockSpec((None, d, f), w_map),
                pl.BlockSpec((None, f, d), w_map),
            ],
            out_specs=pl.BlockSpec(memory_space=pl.ANY),
            scratch_shapes=[
                pltpu.VMEM((tt * SUBLANES, LANES), F32),
                pltpu.VMEM(((tt + 1) * SUBLANES, LANES), F32),
                *[pltpu.VMEM((rows * SUBLANES, LANES), F32) for _ in range(6)],
                pltpu.SemaphoreType.DMA,
            ],
        ),
        out_shape=jax.ShapeDtypeStruct((n_tok * SUBLANES, LANES), F32),
        compiler_params=pltpu.CompilerParams(
            dimension_semantics=("arbitrary", "arbitrary"), vmem_limit_bytes=VMEM_LIMIT),
        name="moe",
    )(off, cnt, tok_s, w_s, h_slab, w_gate, w_up, w_down)


def _dispatch_lists(eidx_t, ew_t, counts, tt):
    n_tok = eidx_t.shape[1]
    n_tiles, n_exp = counts.shape[0], counts.shape[1]
    n_super = n_tok // tt
    t = jnp.arange(n_tok, dtype=jnp.int32)
    key = ((t // tt) * n_exp + eidx_t) * tt + t % tt
    key_s, w_s = lax.sort((key.reshape(-1), ew_t.reshape(-1)), num_keys=1)
    tok_s = (key_s % tt) * SUBLANES
    cnt = counts[:, :, 0].astype(jnp.int32).reshape(n_super, n_tiles // n_super, n_exp).sum(axis=1)
    off = jnp.cumsum(cnt, axis=1) - cnt
    shape = (n_super, 1, tt * TOP_K)
    return tok_s.reshape(shape), w_s.reshape(shape), off.reshape(-1), cnt.reshape(-1)


def _final_kernel(x_ref, sh_ref, ffn_ref, mod_ref, ln_ref, o_ref, *, alpha):
    tm = x_ref.shape[0]
    ffn = sh_ref[...] + _from_slab(ffn_ref, tm)
    z = alpha * x_ref[...] + (1.0 + mod_ref[5:6, :]) * ffn
    o_ref[...] = _layer_norm_rows(z, ln_ref[0:1, :], ln_ref[1:2, :])


def _final(x1, shared, routed_slab, mod, ln, seq, alpha):
    n_tok, d = x1.shape
    tm = min(FIN_TILE, seq)
    assert seq % tm == 0
    return pl.pallas_call(
        functools.partial(_final_kernel, alpha=alpha),
        grid=(n_tok // tm,),
        in_specs=[
            pl.BlockSpec((tm, d), lambda i: (i, 0)),
            pl.BlockSpec((tm, d), lambda i: (i, 0)),
            pl.BlockSpec((tm * SUBLANES, LANES), lambda i: (i, 0)),
            pl.BlockSpec((None, 6, d), lambda i: ((i * tm) // seq, 0, 0)),
            pl.BlockSpec(ln.shape, lambda i: (0, 0)),
        ],
        out_specs=pl.BlockSpec((tm, d), lambda i: (i, 0)),
        out_shape=jax.ShapeDtypeStruct((n_tok, d), F32),
        compiler_params=pltpu.CompilerParams(
            dimension_semantics=("arbitrary",), vmem_limit_bytes=VMEM_LIMIT),
        name="final",
    )(x1, shared, routed_slab, mod, ln)


def kernel(x, c, w_ada, b_ada, w_in, sinks, conv_w, conv_b, conv_ln_g, conv_ln_b, w_o, ln1_g, ln1_b,
           w_router, router_bias, w_gate_e, w_up_e, w_down_e, w_gate_s, w_up_s, w_down_s, ln2_g, ln2_b):
    bsz, seq, d = x.shape
    depth = w_ada.shape[0]
    n_exp = w_router.shape[2]
    n_tok = bsz * seq
    alpha = (2.0 * depth) ** 0.25
    tt = min(SUPER_TILE, n_tok)
    assert n_tok % tt == 0 and tt % min(TOK_TILE, seq) == 0

    for l in range(depth):
        mod = _ada(c, w_ada[l], b_ada[l]).reshape(bsz, 6, d)
        conv_p = jnp.stack([conv_b[l], conv_ln_g[l], conv_ln_b[l]])
        x1 = _mix(x, mod, w_in[l].astype(BF16), sinks[l], conv_w[l], conv_p, w_o[l].astype(BF16),
                  jnp.stack([ln1_g[l], ln1_b[l]]), alpha)
        x1 = x1.reshape(n_tok, d)
        h_slab, shared, eidx_t, ew_t, counts = _ffn_pre(
            x1, mod, w_gate_s[l].astype(BF16), w_up_s[l].astype(BF16), w_down_s[l].astype(BF16),
            w_router[l].T.astype(BF16), router_bias[l].reshape(n_exp, 1), seq)
        tok_s, w_s, off, cnt = _dispatch_lists(eidx_t, ew_t, counts, tt)
        routed = _moe(h_slab, tok_s, w_s, off, cnt, w_gate_e[l].astype(BF16), w_up_e[l].astype(BF16),
                      w_down_e[l].astype(BF16), tt)
        x = _final(x1, shared, routed, mod, jnp.stack([ln2_g[l], ln2_b[l]]), seq, alpha)
        x = x.reshape(bsz, seq, d)
    return x
```

```python
import functools
import math

import jax
import jax.numpy as jnp
from jax import lax
from jax.experimental import pallas as pl
from jax.experimental.pallas import tpu as pltpu

F32 = jnp.float32
BF16 = jnp.bfloat16
NEG_INF = float("-inf")

HEAD_DIM = 64
WINDOW = 128
CONV_KERNEL = 31
CONV_HIST = 32
TOP_K = 8
N_EXPERT_GROUPS = 8
TOPK_EXPERT_GROUPS = 4
ROUTED_SCALE = 2.5
LN_EPS = 1e-5

LANES = 128
SUBLANES = 8
SEQ_TILE = 512
TOK_TILE = 512
FIN_TILE = 512
SUPER_TILE = 4096
MOE_ROWS = 160
MOE_EXPERTS_PER_STEP = 4
MOE_LIST_PAD = 256
RMW_UNROLL = 8
CONV_ROWS = 64
VMEM_LIMIT = 56 * 1024 * 1024


def _alibi_slopes(n_heads):
    return [2.0 ** (-8.0 * (i + 1) / n_heads) for i in range(n_heads)]


def _layer_norm_rows(z, g, b):
    mu = jnp.mean(z, axis=-1, keepdims=True)
    d = z - mu
    var = jnp.mean(d * d, axis=-1, keepdims=True)
    return d * lax.rsqrt(var + LN_EPS) * g + b


def _to_slab(ref, val):
    n = val.shape[0]
    for s in range(SUBLANES):
        ref[pl.ds(s, n, stride=SUBLANES), :] = val[:, s * LANES:(s + 1) * LANES]


def _from_slab(ref, n):
    return jnp.concatenate([ref[pl.ds(s, n, stride=SUBLANES), :] for s in range(SUBLANES)], axis=1)


def _ada_kernel(c_ref, w_ref, b_ref, o_ref):
    c = c_ref[...]
    ca = (c * jax.nn.sigmoid(c)).astype(BF16)
    o_ref[...] = jnp.dot(ca, w_ref[...].astype(BF16), preferred_element_type=F32) + b_ref[...]


def _ada(c, w_ada, b_ada):
    bsz, d = c.shape
    n_out = w_ada.shape[1]
    return pl.pallas_call(
        _ada_kernel,
        grid=(n_out // d,),
        in_specs=[
            pl.BlockSpec((bsz, d), lambda j: (0, 0)),
            pl.BlockSpec((d, d), lambda j: (0, j)),
            pl.BlockSpec((1, d), lambda j: (0, j)),
        ],
        out_specs=pl.BlockSpec((bsz, d), lambda j: (0, j)),
        out_shape=jax.ShapeDtypeStruct((bsz, n_out), F32),
        compiler_params=pltpu.CompilerParams(vmem_limit_bytes=VMEM_LIMIT),
        name="ada",
    )(c, w_ada, b_ada.reshape(1, n_out))


def _mix_kernel(sinks_ref, x_ref, mod_ref, win_ref, convw_ref, convp_ref, wo_ref, ln_ref, o_ref,
                q_ref, ke_ref, ve_ref, glu_ref, gsh_ref, cat_ref, *, ts, aw, alpha):
    s_idx = pl.program_id(1)
    n_heads = aw // HEAD_DIM
    slopes = _alibi_slopes(n_heads)
    cw = cat_ref.shape[1] - aw

    @pl.when(s_idx == 0)
    def _():
        ke_ref[:, 0:WINDOW, :] = jnp.zeros((4, WINDOW, LANES), BF16)
        ve_ref[:, 0:WINDOW, :] = jnp.zeros((4, WINDOW, LANES), BF16)
        glu_ref[0:CONV_HIST, :] = jnp.zeros((CONV_HIST, cw), F32)

    @pl.when(s_idx > 0)
    def _():
        ke_ref[:, 0:WINDOW, :] = ke_ref[:, ts:ts + WINDOW, :]
        ve_ref[:, 0:WINDOW, :] = ve_ref[:, ts:ts + WINDOW, :]
        glu_ref[0:CONV_HIST, :] = glu_ref[ts:ts + CONV_HIST, :]

    x = x_ref[...]
    h = (x * (1.0 + mod_ref[1:2, :]) + mod_ref[0:1, :]).astype(BF16)

    q = jnp.dot(h, win_ref[:, 0:aw], preferred_element_type=F32)
    q_ref[...] = (q * (1.0 / math.sqrt(HEAD_DIM))).astype(BF16)
    kv = jnp.dot(h, win_ref[:, aw:aw + 2 * LANES], preferred_element_type=F32)
    lo = lax.broadcasted_iota(jnp.int32, (ts, LANES), 1) < HEAD_DIM
    for dst, t in ((ke_ref, kv[:, 0:LANES]), (ve_ref, kv[:, LANES:2 * LANES])):
        t_r = pltpu.roll(t, HEAD_DIM, axis=1)
        dst[0, WINDOW:WINDOW + ts, :] = jnp.where(lo, t, 0.0).astype(BF16)
        dst[1, WINDOW:WINDOW + ts, :] = jnp.where(lo, 0.0, t_r).astype(BF16)
        dst[2, WINDOW:WINDOW + ts, :] = jnp.where(lo, t_r, 0.0).astype(BF16)
        dst[3, WINDOW:WINDOW + ts, :] = jnp.where(lo, 0.0, t).astype(BF16)
    u0 = aw + 2 * LANES
    ga = jnp.dot(h, win_ref[:, u0:u0 + cw], preferred_element_type=F32)
    gb = jnp.dot(h, win_ref[:, u0 + cw:u0 + 2 * cw], preferred_element_type=F32)
    glu_ref[CONV_HIST:CONV_HIST + ts, :] = ga * jax.nn.sigmoid(gb)

    qi = lax.broadcasted_iota(jnp.int32, (WINDOW, 2 * WINDOW), 0)
    kj = lax.broadcasted_iota(jnp.int32, (WINDOW, 2 * WINDOW), 1)
    dist = WINDOW + qi - kj
    band = (dist >= 0) & (dist < WINDOW)
    neg_dist = -dist.astype(F32)
    bias_any = jnp.where(band, neg_dist, NEG_INF)
    bias_first = jnp.where(band & ((kj >= WINDOW) | (s_idx > 0)), neg_dist, NEG_INF)
    for i in range(ts // WINDOW):
        r0 = i * WINDOW
        bias = bias_first if i == 0 else bias_any
        for pair in range(aw // LANES):
            g = (2 * pair) // (n_heads // 2)
            qp = q_ref[r0:r0 + WINDOW, pair * LANES:(pair + 1) * LANES]
            out_pair = None
            for par in range(2):
                hd = 2 * pair + par
                kk = ke_ref[2 * g + par, r0:r0 + 2 * WINDOW, :]
                s = lax.dot_general(qp, kk, (((1,), (1,)), ((), ())), preferred_element_type=F32)
                s = s + slopes[hd] * bias
                sink = sinks_ref[hd]
                m = jnp.maximum(jnp.max(s, axis=-1, keepdims=True), sink)
                p = jnp.exp(s - m)
                denom = jnp.sum(p, axis=-1, keepdims=True) + jnp.exp(sink - m)
                vv = ve_ref[2 * g + par, r0:r0 + 2 * WINDOW, :]
                o = jnp.dot(p.astype(BF16), vv, preferred_element_type=F32) * (1.0 / denom)
                out_pair = o if out_pair is None else out_pair + o
            cat_ref[r0:r0 + WINDOW, pair * LANES:(pair + 1) * LANES] = out_pair.astype(BF16)

    conv_b = convp_ref[0:1, :]
    cln_g = convp_ref[1:2, :]
    cln_b = convp_ref[2:3, :]
    off = CONV_HIST - (CONV_KERNEL - 1)
    n_sh = gsh_ref.shape[1]
    for p in range(1, SUBLANES):
        gsh_ref[p - 1] = glu_ref[p:p + n_sh, :]
    for c in range(ts // CONV_ROWS):
        c0 = c * CONV_ROWS
        acc = jnp.broadcast_to(conv_b, (CONV_ROWS, cw))
        for j in range(CONV_KERNEL):
            a, p = divmod(off + j, SUBLANES)
            r0 = c0 + a * SUBLANES
            tap = glu_ref[r0:r0 + CONV_ROWS, :] if p == 0 else gsh_ref[p - 1, r0:r0 + CONV_ROWS, :]
            acc = acc + tap * convw_ref[j:j + 1, :]
        yn = _layer_norm_rows(acc, cln_g, cln_b)
        cat_ref[c0:c0 + CONV_ROWS, aw:aw + cw] = (yn * jax.nn.sigmoid(yn)).astype(BF16)

    mix = jnp.dot(cat_ref[...], wo_ref[...], preferred_element_type=F32)
    z = alpha * x + (1.0 + mod_ref[2:3, :]) * mix
    o_ref[...] = _layer_norm_rows(z, ln_ref[0:1, :], ln_ref[1:2, :])


def _mix(x, mod, w_in, sinks, conv_w, conv_p, w_o, ln, alpha):
    bsz, seq, d = x.shape
    cw = conv_w.shape[1]
    aw = d - cw
    ts = min(SEQ_TILE, seq)
    assert seq % ts == 0 and ts % WINDOW == 0 and aw % LANES == 0
    assert (aw // HEAD_DIM) // 4 == 2, "kernel packs exactly two KV heads into one lane group"
    assert w_in.shape[1] == aw + 2 * LANES + 2 * cw
    kern = functools.partial(_mix_kernel, ts=ts, aw=aw, alpha=alpha)
    const = lambda b, s: (0, 0)
    return pl.pallas_call(
        kern,
        grid=(bsz, seq // ts),
        in_specs=[
            pl.BlockSpec(memory_space=pltpu.SMEM),
            pl.BlockSpec((None, ts, d), lambda b, s: (b, s, 0)),
            pl.BlockSpec((None, 6, d), lambda b, s: (b, 0, 0)),
            pl.BlockSpec(w_in.shape, const),
            pl.BlockSpec(conv_w.shape, const),
            pl.BlockSpec(conv_p.shape, const),
            pl.BlockSpec(w_o.shape, const),
            pl.BlockSpec(ln.shape, const),
        ],
        out_specs=pl.BlockSpec((None, ts, d), lambda b, s: (b, s, 0)),
        out_shape=jax.ShapeDtypeStruct((bsz, seq, d), F32),
        scratch_shapes=[
            pltpu.VMEM((ts, aw), BF16),
            pltpu.VMEM((4, ts + WINDOW, LANES), BF16),
            pltpu.VMEM((4, ts + WINDOW, LANES), BF16),
            pltpu.VMEM((ts + CONV_HIST, cw), F32),
            pltpu.VMEM((SUBLANES - 1, ts + CONV_HIST - SUBLANES, cw), F32),
            pltpu.VMEM((ts, d), BF16),
        ],
        compiler_params=pltpu.CompilerParams(
            dimension_semantics=("arbitrary", "arbitrary"), vmem_limit_bytes=VMEM_LIMIT),
        name="mix",
    )(sinks, x, mod, w_in, conv_w, conv_p, w_o, ln)


def _ffn_pre_kernel(x_ref, mod_ref, wgs_ref, wus_ref, wds_ref, wrt_ref, rb_ref,
                    h_ref, sh_ref, ei_ref, ew_ref, cnt_ref):
    tm = x_ref.shape[0]
    n_exp = wrt_ref.shape[0]
    per = n_exp // N_EXPERT_GROUPS
    hf = x_ref[...] * (1.0 + mod_ref[4:5, :]) + mod_ref[3:4, :]
    _to_slab(h_ref, hf)
    h = hf.astype(BF16)

    gate = jnp.dot(h, wgs_ref[...], preferred_element_type=F32)
    up = jnp.dot(h, wus_ref[...], preferred_element_type=F32)
    act = (gate * jax.nn.sigmoid(gate) * up).astype(BF16)
    sh_ref[...] = jnp.dot(act, wds_ref[...], preferred_element_type=F32)

    logits = lax.dot_general(wrt_ref[...], h, (((1,), (1,)), ((), ())), preferred_element_type=F32)
    scores = jax.nn.sigmoid(logits)
    sel = scores + rb_ref[...]

    iota_p = lax.broadcasted_iota(jnp.int32, (per, tm), 0).astype(F32)
    gs_rows = []
    for g in range(N_EXPERT_GROUPS):
        blk = sel[g * per:(g + 1) * per, :]
        m1 = jnp.max(blk, axis=0, keepdims=True)
        i1 = jnp.min(jnp.where(blk == m1, iota_p, float(per)), axis=0, keepdims=True)
        m2 = jnp.max(jnp.where(iota_p == i1, NEG_INF, blk), axis=0, keepdims=True)
        gs_rows.append(m1 + m2)
    gs = jnp.concatenate(gs_rows, axis=0)
    iota_g = lax.broadcasted_iota(jnp.int32, (N_EXPERT_GROUPS, tm), 0).astype(F32)
    gmask = jnp.zeros((N_EXPERT_GROUPS, tm), jnp.bool_)
    for _ in range(TOPK_EXPERT_GROUPS):
        m = jnp.max(gs, axis=0, keepdims=True)
        gi = jnp.min(jnp.where(gs == m, iota_g, float(N_EXPERT_GROUPS)), axis=0, keepdims=True)
        hit = iota_g == gi
        gmask = gmask | hit
        gs = jnp.where(hit, NEG_INF, gs)
    emask = jnp.concatenate(
        [jnp.broadcast_to(gmask[g:g + 1, :], (per, tm)) for g in range(N_EXPERT_GROUPS)], axis=0)
    cand = jnp.where(emask, sel, NEG_INF)

    iota_e = lax.broadcasted_iota(jnp.int32, (n_exp, tm), 0).astype(F32)
    idx_rows, w_rows = [], []
    for _ in range(TOP_K):
        m = jnp.max(cand, axis=0, keepdims=True)
        ei = jnp.min(jnp.where(cand == m, iota_e, float(n_exp)), axis=0, keepdims=True)
        hit = iota_e == ei
        w_rows.append(jnp.sum(jnp.where(hit, scores, 0.0), axis=0, keepdims=True))
        idx_rows.append(ei)
        cand = jnp.where(hit, NEG_INF, cand)
    wk = jnp.concatenate(w_rows, axis=0)
    ew_ref[...] = wk / jnp.sum(wk, axis=0, keepdims=True) * ROUTED_SCALE
    ei_ref[...] = jnp.concatenate(idx_rows, axis=0).astype(jnp.int32)
    chosen = jnp.where(emask & (cand == NEG_INF), 1.0, 0.0).astype(BF16)
    cnt_ref[...] = jnp.dot(chosen, jnp.ones((tm, LANES), BF16), preferred_element_type=F32)


def _ffn_pre(x1, mod, wgs, wus, wds, wrt, rbias, seq):
    n_tok, d = x1.shape
    tm = min(TOK_TILE, seq)
    assert seq % tm == 0 and n_tok % tm == 0 and d == SUBLANES * LANES
    n_exp = wrt.shape[0]
    const = lambda i: (0, 0)
    return pl.pallas_call(
        _ffn_pre_kernel,
        grid=(n_tok // tm,),
        in_specs=[
            pl.BlockSpec((tm, d), lambda i: (i, 0)),
            pl.BlockSpec((None, 6, d), lambda i: ((i * tm) // seq, 0, 0)),
            pl.BlockSpec(wgs.shape, const),
            pl.BlockSpec(wus.shape, const),
            pl.BlockSpec(wds.shape, const),
            pl.BlockSpec(wrt.shape, const),
            pl.BlockSpec(rbias.shape, const),
        ],
        out_specs=[
            pl.BlockSpec((tm * SUBLANES, LANES), lambda i: (i, 0)),
            pl.BlockSpec((tm, d), lambda i: (i, 0)),
            pl.BlockSpec((TOP_K, tm), lambda i: (0, i)),
            pl.BlockSpec((TOP_K, tm), lambda i: (0, i)),
            pl.BlockSpec((None, n_exp, LANES), lambda i: (i, 0, 0)),
        ],
        out_shape=[
            jax.ShapeDtypeStruct((n_tok * SUBLANES, LANES), F32),
            jax.ShapeDtypeStruct((n_tok, d), F32),
            jax.ShapeDtypeStruct((TOP_K, n_tok), jnp.int32),
            jax.ShapeDtypeStruct((TOP_K, n_tok), F32),
            jax.ShapeDtypeStruct((n_tok // tm, n_exp, LANES), F32),
        ],
        compiler_params=pltpu.CompilerParams(
            dimension_semantics=("arbitrary",), vmem_limit_bytes=VMEM_LIMIT),
        name="ffn_pre",
    )(x1, mod, wgs, wus, wds, wrt, rbias)


def _moe_kernel(off_ref, cnt_ref, tok_ref, w_ref, h_hbm, wg_ref, wu_ref, wd_ref, out_hbm,
                h_vmem, acc_ref, xa_ref, xb_ref, xd_ref, ya_ref, yb_ref, yd_ref, sem, *, tt):
    j = pl.program_id(0)
    i = pl.program_id(1)
    n_grp = pl.num_programs(1)
    n_exp = n_grp * MOE_EXPERTS_PER_STEP
    rows = xa_ref.shape[0] // SUBLANES
    last = pl.num_programs(0) * n_exp - 1
    row_iota = lax.broadcasted_iota(jnp.int32, (rows, 1), 0)

    def slab(r):
        if isinstance(r, int):
            return pl.ds(r * SUBLANES, SUBLANES)
        return pl.ds(pl.multiple_of(r * SUBLANES, SUBLANES), SUBLANES)

    def gather_rows(x_ref, base, row_ids):
        for r in row_ids:
            x_ref[slab(r), :] = h_vmem[pl.ds(pl.multiple_of(tok_ref[0, 0, base + r], SUBLANES), SUBLANES), :]

    def scatter_rows(y_ref, base, row_ids):
        for g in range(0, len(row_ids), RMW_UNROLL):
            updates = []
            for r in row_ids[g:g + RMW_UNROLL]:
                dst = pl.ds(pl.multiple_of(tok_ref[0, 0, base + r], SUBLANES), SUBLANES)
                updates.append((dst, acc_ref[dst, :] + w_ref[0, 0, base + r] * y_ref[slab(r), :]))
            for dst, val in reversed(updates):
                acc_ref[dst, :] = val

    def expert_mlp(x_ref, y_ref, k, n_valid):
        x = _from_slab(x_ref, rows).astype(BF16)
        gate = jnp.dot(x, wg_ref[k], preferred_element_type=F32)
        up = jnp.dot(x, wu_ref[k], preferred_element_type=F32)
        act = (gate * jax.nn.sigmoid(gate) * up).astype(BF16)
        y = jnp.dot(act, wd_ref[k], preferred_element_type=F32)
        _to_slab(y_ref, jnp.where(row_iota < n_valid, y, 0.0))

    def gather_loop(x_ref, base):
        def body(g, carry):
            gather_rows(x_ref, base, [g * SUBLANES + u for u in range(SUBLANES)])
            return carry
        lax.fori_loop(0, rows // SUBLANES, body, 0)

    def scatter_loop(y_ref, base):
        def body(g, carry):
            scatter_rows(y_ref, base, [g * RMW_UNROLL + u for u in range(RMW_UNROLL)])
            return carry
        lax.fori_loop(0, rows // RMW_UNROLL, body, 0)

    idx0 = j * n_exp + i * MOE_EXPERTS_PER_STEP

    @pl.when(i == 0)
    def _():
        load = pltpu.make_async_copy(h_hbm.at[pl.ds(j * (tt * SUBLANES), tt * SUBLANES)], h_vmem, sem)
        load.start()
        acc_ref[...] = jnp.zeros_like(acc_ref)
        ya_ref[...] = jnp.zeros_like(ya_ref)
        yb_ref[...] = jnp.zeros_like(yb_ref)
        load.wait()
        gather_loop(xa_ref, off_ref[idx0])

    static_rows = list(range(rows))
    bufs = ((xa_ref, xb_ref, ya_ref, yb_ref), (xb_ref, xa_ref, yb_ref, ya_ref))
    for k in range(MOE_EXPERTS_PER_STEP):
        x_cur, x_next, y_cur, y_prev = bufs[k % 2]
        idx = idx0 + k
        scatter_rows(y_prev, off_ref[jnp.maximum(idx - 1, 0)], static_rows)
        expert_mlp(x_cur, y_cur, k, cnt_ref[idx])
        gather_rows(x_next, off_ref[jnp.minimum(idx + 1, last)], static_rows)

    for k in range(MOE_EXPERTS_PER_STEP):
        off = off_ref[idx0 + k]
        cnt = cnt_ref[idx0 + k]

        def extra(c, carry):
            gather_loop(xd_ref, off + c * rows)
            expert_mlp(xd_ref, yd_ref, k, cnt - c * rows)
            scatter_loop(yd_ref, off + c * rows)
            return carry

        lax.fori_loop(1, (cnt + rows - 1) // rows, extra, 0)

    @pl.when(i == n_grp - 1)
    def _():
        scatter_loop(bufs[(MOE_EXPERTS_PER_STEP - 1) % 2][2], off_ref[idx0 + MOE_EXPERTS_PER_STEP - 1])
        store = pltpu.make_async_copy(acc_ref, out_hbm.at[pl.ds(j * (tt * SUBLANES), tt * SUBLANES)], sem)
        store.start()
        store.wait()


def _moe(h_slab, tok_s, w_s, off, cnt, w_gate, w_up, w_down, tt):
    n_tok = h_slab.shape[0] // SUBLANES
    n_exp, d, f = w_gate.shape
    n_super = n_tok // tt
    rows = MOE_ROWS
    eb = MOE_EXPERTS_PER_STEP
    assert rows % (2 * SUBLANES) == 0 and rows % RMW_UNROLL == 0 and n_exp % eb == 0 and eb % 2 == 0
    assert tok_s.shape[2] >= tt * TOP_K + rows
    w_map = lambda j, i, off, cnt: (i, 0, 0)
    lst_map = lambda j, i, off, cnt: (j, 0, 0)
    return pl.pallas_call(
        functools.partial(_moe_kernel, tt=tt),
        grid_spec=pltpu.PrefetchScalarGridSpec(
            num_scalar_prefetch=2,
            grid=(n_super, n_exp // eb),
            in_specs=[
                pl.BlockSpec((1, 1, tok_s.shape[2]), lst_map, memory_space=pltpu.SMEM),
                pl.BlockSpec((1, 1, w_s.shape[2]), lst_map, memory_space=pltpu.SMEM),
                pl.BlockSpec(memory_space=pl.ANY),
                pl.BlockSpec((eb, d, f), w_map),
                pl.BlockSpec((eb, d, f), w_map),
                pl.BlockSpec((eb, f, d), w_map),
            ],
            out_specs=pl.BlockSpec(memory_space=pl.ANY),
            scratch_shapes=[
                pltpu.VMEM((tt * SUBLANES, LANES), F32),
                pltpu.VMEM((tt * SUBLANES, LANES), F32),
                *[pltpu.VMEM((rows * SUBLANES, LANES), F32) for _ in range(6)],
                pltpu.SemaphoreType.DMA,
            ],
        ),
        out_shape=jax.ShapeDtypeStruct((n_tok * SUBLANES, LANES), F32),
        compiler_params=pltpu.CompilerParams(
            dimension_semantics=("arbitrary", "arbitrary"), vmem_limit_bytes=VMEM_LIMIT),
        name="moe",
    )(off, cnt, tok_s, w_s, h_slab, w_gate, w_up, w_down)


def _dispatch_lists(eidx_t, ew_t, counts, tt):
    n_tok = eidx_t.shape[1]
    n_tiles, n_exp = counts.shape[0], counts.shape[1]
    n_super = n_tok // tt
    t = jnp.arange(n_tok, dtype=jnp.int32)
    key = ((t // tt) * n_exp + eidx_t) * tt + t % tt
    key_s, w_s = lax.sort((key.reshape(-1), ew_t.reshape(-1)), num_keys=1)
    tok_s = (key_s % tt) * SUBLANES
    pad = ((0, 0), (0, 0), (0, MOE_LIST_PAD))
    tok_s = jnp.pad(tok_s.reshape(n_super, 1, tt * TOP_K), pad)
    w_s = jnp.pad(w_s.reshape(n_super, 1, tt * TOP_K), pad)
    cnt = counts[:, :, 0].astype(jnp.int32).reshape(n_super, n_tiles // n_super, n_exp).sum(axis=1)
    off = jnp.cumsum(cnt, axis=1) - cnt
    return tok_s, w_s, off.reshape(-1), cnt.reshape(-1)


def _final_kernel(x_ref, sh_ref, ffn_ref, mod_ref, ln_ref, o_ref, *, alpha):
    tm = x_ref.shape[0]
    ffn = sh_ref[...] + _from_slab(ffn_ref, tm)
    z = alpha * x_ref[...] + (1.0 + mod_ref[5:6, :]) * ffn
    o_ref[...] = _layer_norm_rows(z, ln_ref[0:1, :], ln_ref[1:2, :])


def _final(x1, shared, routed_slab, mod, ln, seq, alpha):
    n_tok, d = x1.shape
    tm = min(FIN_TILE, seq)
    assert seq % tm == 0
    return pl.pallas_call(
        functools.partial(_final_kernel, alpha=alpha),
        grid=(n_tok // tm,),
        in_specs=[
            pl.BlockSpec((tm, d), lambda i: (i, 0)),
            pl.BlockSpec((tm, d), lambda i: (i, 0)),
            pl.BlockSpec((tm * SUBLANES, LANES), lambda i: (i, 0)),
            pl.BlockSpec((None, 6, d), lambda i: ((i * tm) // seq, 0, 0)),
            pl.BlockSpec(ln.shape, lambda i: (0, 0)),
        ],
        out_specs=pl.BlockSpec((tm, d), lambda i: (i, 0)),
        out_shape=jax.ShapeDtypeStruct((n_tok, d), F32),
        compiler_params=pltpu.CompilerParams(
            dimension_semantics=("arbitrary",), vmem_limit_bytes=VMEM_LIMIT),
        name="final",
    )(x1, shared, routed_slab, mod, ln)


def kernel(x, c, w_ada, b_ada, w_in, sinks, conv_w, conv_b, conv_ln_g, conv_ln_b, w_o, ln1_g, ln1_b,
           w_router, router_bias, w_gate_e, w_up_e, w_down_e, w_gate_s, w_up_s, w_down_s, ln2_g, ln2_b):
    bsz, seq, d = x.shape
    depth = w_ada.shape[0]
    n_exp = w_router.shape[2]
    n_tok = bsz * seq
    alpha = (2.0 * depth) ** 0.25
    tt = min(SUPER_TILE, n_tok)
    assert n_tok % tt == 0 and tt % min(TOK_TILE, seq) == 0

    for l in range(depth):
        mod = _ada(c, w_ada[l], b_ada[l]).reshape(bsz, 6, d)
        conv_p = jnp.stack([conv_b[l], conv_ln_g[l], conv_ln_b[l]])
        x1 = _mix(x, mod, w_in[l].astype(BF16), sinks[l], conv_w[l], conv_p, w_o[l].astype(BF16),
                  jnp.stack([ln1_g[l], ln1_b[l]]), alpha)
        x1 = x1.reshape(n_tok, d)
        h_slab, shared, eidx_t, ew_t, counts = _ffn_pre(
            x1, mod, w_gate_s[l].astype(BF16), w_up_s[l].astype(BF16), w_down_s[l].astype(BF16),
            w_router[l].T.astype(BF16), router_bias[l].reshape(n_exp, 1), seq)
        tok_s, w_s, off, cnt = _dispatch_lists(eidx_t, ew_t, counts, tt)
        routed = _moe(h_slab, tok_s, w_s, off, cnt, w_gate_e[l].astype(BF16), w_up_e[l].astype(BF16),
                      w_down_e[l].astype(BF16), tt)
        x = _final(x1, shared, routed, mod, jnp.stack([ln2_g[l], ln2_b[l]]), seq, alpha)
        x = x.reshape(bsz, seq, d)
    return x
```

```python
import functools
import math

import jax
import jax.numpy as jnp
from jax import lax
from jax.experimental import pallas as pl
from jax.experimental.pallas import tpu as pltpu

F32 = jnp.float32
BF16 = jnp.bfloat16
NEG_INF = float("-inf")

HEAD_DIM = 64
WINDOW = 128
CONV_KERNEL = 31
CONV_HIST = 32
TOP_K = 8
N_EXPERT_GROUPS = 8
TOPK_EXPERT_GROUPS = 4
ROUTED_SCALE = 2.5
LN_EPS = 1e-5

LANES = 128
SUBLANES = 8
SEQ_TILE = 512
TOK_TILE = 512
FIN_TILE = 512
SUPER_TILE = 4096
MOE_ROWS = 192
MOE_EXPERTS_PER_STEP = 4
MOE_LIST_PAD = 256
RMW_UNROLL = 8
CONV_ROWS = 64
VMEM_LIMIT = 56 * 1024 * 1024


def _alibi_slopes(n_heads):
    return [2.0 ** (-8.0 * (i + 1) / n_heads) for i in range(n_heads)]


def _layer_norm_rows(z, g, b):
    mu = jnp.mean(z, axis=-1, keepdims=True)
    d = z - mu
    var = jnp.mean(d * d, axis=-1, keepdims=True)
    return d * lax.rsqrt(var + LN_EPS) * g + b


def _to_slab(ref, val):
    n = val.shape[0]
    for s in range(SUBLANES):
        ref[pl.ds(s, n, stride=SUBLANES), :] = val[:, s * LANES:(s + 1) * LANES]


def _from_slab(ref, n):
    return jnp.concatenate([ref[pl.ds(s, n, stride=SUBLANES), :] for s in range(SUBLANES)], axis=1)


def _ada_kernel(c_ref, w_ref, b_ref, o_ref):
    c = c_ref[...]
    ca = (c * jax.nn.sigmoid(c)).astype(BF16)
    o_ref[...] = jnp.dot(ca, w_ref[...].astype(BF16), preferred_element_type=F32) + b_ref[...]


def _ada(c, w_ada, b_ada):
    bsz, d = c.shape
    n_out = w_ada.shape[1]
    return pl.pallas_call(
        _ada_kernel,
        grid=(n_out // d,),
        in_specs=[
            pl.BlockSpec((bsz, d), lambda j: (0, 0)),
            pl.BlockSpec((d, d), lambda j: (0, j)),
            pl.BlockSpec((1, d), lambda j: (0, j)),
        ],
        out_specs=pl.BlockSpec((bsz, d), lambda j: (0, j)),
        out_shape=jax.ShapeDtypeStruct((bsz, n_out), F32),
        compiler_params=pltpu.CompilerParams(vmem_limit_bytes=VMEM_LIMIT),
        name="ada",
    )(c, w_ada, b_ada.reshape(1, n_out))


def _mix_kernel(sinks_ref, x_ref, mod_ref, win_ref, convw_ref, convp_ref, wo_ref, ln_ref, wcast_ref,
                o_ref, wcast_out_ref, q_ref, ke_ref, ve_ref, glu_ref, gsh_ref, cat_ref, *, ts, aw, alpha):
    wcast_out_ref[...] = wcast_ref[...].astype(BF16)
    s_idx = pl.program_id(1)
    n_heads = aw // HEAD_DIM
    slopes = _alibi_slopes(n_heads)
    cw = cat_ref.shape[1] - aw

    @pl.when(s_idx == 0)
    def _():
        ke_ref[:, 0:WINDOW, :] = jnp.zeros((4, WINDOW, LANES), BF16)
        ve_ref[:, 0:WINDOW, :] = jnp.zeros((4, WINDOW, LANES), BF16)
        glu_ref[0:CONV_HIST, :] = jnp.zeros((CONV_HIST, cw), F32)

    @pl.when(s_idx > 0)
    def _():
        ke_ref[:, 0:WINDOW, :] = ke_ref[:, ts:ts + WINDOW, :]
        ve_ref[:, 0:WINDOW, :] = ve_ref[:, ts:ts + WINDOW, :]
        glu_ref[0:CONV_HIST, :] = glu_ref[ts:ts + CONV_HIST, :]

    x = x_ref[...]
    h = (x * (1.0 + mod_ref[1:2, :]) + mod_ref[0:1, :]).astype(BF16)

    q = jnp.dot(h, win_ref[:, 0:aw], preferred_element_type=F32)
    q_ref[...] = (q * (1.0 / math.sqrt(HEAD_DIM))).astype(BF16)
    kv = jnp.dot(h, win_ref[:, aw:aw + 2 * LANES], preferred_element_type=F32)
    lo = lax.broadcasted_iota(jnp.int32, (ts, LANES), 1) < HEAD_DIM
    for dst, t in ((ke_ref, kv[:, 0:LANES]), (ve_ref, kv[:, LANES:2 * LANES])):
        t_r = pltpu.roll(t, HEAD_DIM, axis=1)
        dst[0, WINDOW:WINDOW + ts, :] = jnp.where(lo, t, 0.0).astype(BF16)
        dst[1, WINDOW:WINDOW + ts, :] = jnp.where(lo, 0.0, t_r).astype(BF16)
        dst[2, WINDOW:WINDOW + ts, :] = jnp.where(lo, t_r, 0.0).astype(BF16)
        dst[3, WINDOW:WINDOW + ts, :] = jnp.where(lo, 0.0, t).astype(BF16)
    u0 = aw + 2 * LANES
    ga = jnp.dot(h, win_ref[:, u0:u0 + cw], preferred_element_type=F32)
    gb = jnp.dot(h, win_ref[:, u0 + cw:u0 + 2 * cw], preferred_element_type=F32)
    glu_ref[CONV_HIST:CONV_HIST + ts, :] = ga * jax.nn.sigmoid(gb)

    qi = lax.broadcasted_iota(jnp.int32, (WINDOW, 2 * WINDOW), 0)
    kj = lax.broadcasted_iota(jnp.int32, (WINDOW, 2 * WINDOW), 1)
    dist = WINDOW + qi - kj
    band = (dist >= 0) & (dist < WINDOW)
    neg_dist = -dist.astype(F32)
    bias_any = jnp.where(band, neg_dist, NEG_INF)
    bias_first = jnp.where(band & ((kj >= WINDOW) | (s_idx > 0)), neg_dist, NEG_INF)
    for i in range(ts // WINDOW):
        r0 = i * WINDOW
        bias = bias_first if i == 0 else bias_any
        for pair in range(aw // LANES):
            g = (2 * pair) // (n_heads // 2)
            qp = q_ref[r0:r0 + WINDOW, pair * LANES:(pair + 1) * LANES]
            out_pair = None
            for par in range(2):
                hd = 2 * pair + par
                kk = ke_ref[2 * g + par, r0:r0 + 2 * WINDOW, :]
                s = lax.dot_general(qp, kk, (((1,), (1,)), ((), ())), preferred_element_type=F32)
                s = s + slopes[hd] * bias
                sink = sinks_ref[hd]
                m = jnp.maximum(jnp.max(s, axis=-1, keepdims=True), sink)
                p = jnp.exp(s - m)
                denom = jnp.sum(p, axis=-1, keepdims=True) + jnp.exp(sink - m)
                vv = ve_ref[2 * g + par, r0:r0 + 2 * WINDOW, :]
                o = jnp.dot(p.astype(BF16), vv, preferred_element_type=F32) * (1.0 / denom)
                out_pair = o if out_pair is None else out_pair + o
            cat_ref[r0:r0 + WINDOW, pair * LANES:(pair + 1) * LANES] = out_pair.astype(BF16)

    conv_b = convp_ref[0:1, :]
    cln_g = convp_ref[1:2, :]
    cln_b = convp_ref[2:3, :]
    off = CONV_HIST - (CONV_KERNEL - 1)
    n_sh = gsh_ref.shape[1]
    for p in range(1, SUBLANES):
        gsh_ref[p - 1] = glu_ref[p:p + n_sh, :]
    for c in range(ts // CONV_ROWS):
        c0 = c * CONV_ROWS
        acc = jnp.broadcast_to(conv_b, (CONV_ROWS, cw))
        for j in range(CONV_KERNEL):
            a, p = divmod(off + j, SUBLANES)
            r0 = c0 + a * SUBLANES
            tap = glu_ref[r0:r0 + CONV_ROWS, :] if p == 0 else gsh_ref[p - 1, r0:r0 + CONV_ROWS, :]
            acc = acc + tap * convw_ref[j:j + 1, :]
        yn = _layer_norm_rows(acc, cln_g, cln_b)
        cat_ref[c0:c0 + CONV_ROWS, aw:aw + cw] = (yn * jax.nn.sigmoid(yn)).astype(BF16)

    mix = jnp.dot(cat_ref[...], wo_ref[...], preferred_element_type=F32)
    z = alpha * x + (1.0 + mod_ref[2:3, :]) * mix
    o_ref[...] = _layer_norm_rows(z, ln_ref[0:1, :], ln_ref[1:2, :])


def _mix(x, mod, w_in, sinks, conv_w, conv_p, w_o, ln, w_cast, alpha):
    bsz, seq, d = x.shape
    cw = conv_w.shape[1]
    aw = d - cw
    ts = min(SEQ_TILE, seq)
    n_seq = seq // ts
    assert seq % ts == 0 and ts % WINDOW == 0 and aw % LANES == 0
    assert (aw // HEAD_DIM) // 4 == 2, "kernel packs exactly two KV heads into one lane group"
    assert w_in.shape[1] == aw + 2 * LANES + 2 * cw
    assert w_cast.shape[0] % (bsz * n_seq) == 0
    cast_blk = (w_cast.shape[0] // (bsz * n_seq),) + w_cast.shape[1:]
    cast_map = lambda b, s: (b * n_seq + s, 0, 0)
    kern = functools.partial(_mix_kernel, ts=ts, aw=aw, alpha=alpha)
    const = lambda b, s: (0, 0)
    return pl.pallas_call(
        kern,
        grid=(bsz, seq // ts),
        in_specs=[
            pl.BlockSpec(memory_space=pltpu.SMEM),
            pl.BlockSpec((None, ts, d), lambda b, s: (b, s, 0)),
            pl.BlockSpec((None, 6, d), lambda b, s: (b, 0, 0)),
            pl.BlockSpec(w_in.shape, const),
            pl.BlockSpec(conv_w.shape, const),
            pl.BlockSpec(conv_p.shape, const),
            pl.BlockSpec(w_o.shape, const),
            pl.BlockSpec(ln.shape, const),
            pl.BlockSpec(cast_blk, cast_map),
        ],
        out_specs=[pl.BlockSpec((None, ts, d), lambda b, s: (b, s, 0)), pl.BlockSpec(cast_blk, cast_map)],
        out_shape=[jax.ShapeDtypeStruct((bsz, seq, d), F32), jax.ShapeDtypeStruct(w_cast.shape, BF16)],
        scratch_shapes=[
            pltpu.VMEM((ts, aw), BF16),
            pltpu.VMEM((4, ts + WINDOW, LANES), BF16),
            pltpu.VMEM((4, ts + WINDOW, LANES), BF16),
            pltpu.VMEM((ts + CONV_HIST, cw), F32),
            pltpu.VMEM((SUBLANES - 1, ts + CONV_HIST - SUBLANES, cw), F32),
            pltpu.VMEM((ts, d), BF16),
        ],
        compiler_params=pltpu.CompilerParams(
            dimension_semantics=("arbitrary", "arbitrary"), vmem_limit_bytes=VMEM_LIMIT),
        name="mix",
    )(sinks, x, mod, w_in, conv_w, conv_p, w_o, ln, w_cast)


def _ffn_pre_kernel(x_ref, mod_ref, wgs_ref, wus_ref, wds_ref, wrt_ref, rb_ref, wga_ref, wua_ref,
                    h_ref, sh_ref, ei_ref, ew_ref, cnt_ref, wgo_ref, wuo_ref):
    wgo_ref[...] = wga_ref[...].astype(BF16)
    wuo_ref[...] = wua_ref[...].astype(BF16)
    tm = x_ref.shape[0]
    n_exp = wrt_ref.shape[0]
    per = n_exp // N_EXPERT_GROUPS
    hf = x_ref[...] * (1.0 + mod_ref[4:5, :]) + mod_ref[3:4, :]
    _to_slab(h_ref, hf)
    h = hf.astype(BF16)

    gate = jnp.dot(h, wgs_ref[...], preferred_element_type=F32)
    up = jnp.dot(h, wus_ref[...], preferred_element_type=F32)
    act = (gate * jax.nn.sigmoid(gate) * up).astype(BF16)
    sh_ref[...] = jnp.dot(act, wds_ref[...], preferred_element_type=F32)

    logits = lax.dot_general(wrt_ref[...], h, (((1,), (1,)), ((), ())), preferred_element_type=F32)
    scores = jax.nn.sigmoid(logits)
    sel = scores + rb_ref[...]

    iota_p = lax.broadcasted_iota(jnp.int32, (per, tm), 0).astype(F32)
    gs_rows = []
    for g in range(N_EXPERT_GROUPS):
        blk = sel[g * per:(g + 1) * per, :]
        m1 = jnp.max(blk, axis=0, keepdims=True)
        i1 = jnp.min(jnp.where(blk == m1, iota_p, float(per)), axis=0, keepdims=True)
        m2 = jnp.max(jnp.where(iota_p == i1, NEG_INF, blk), axis=0, keepdims=True)
        gs_rows.append(m1 + m2)
    gs = jnp.concatenate(gs_rows, axis=0)
    iota_g = lax.broadcasted_iota(jnp.int32, (N_EXPERT_GROUPS, tm), 0).astype(F32)
    gmask = jnp.zeros((N_EXPERT_GROUPS, tm), jnp.bool_)
    for _ in range(TOPK_EXPERT_GROUPS):
        m = jnp.max(gs, axis=0, keepdims=True)
        gi = jnp.min(jnp.where(gs == m, iota_g, float(N_EXPERT_GROUPS)), axis=0, keepdims=True)
        hit = iota_g == gi
        gmask = gmask | hit
        gs = jnp.where(hit, NEG_INF, gs)
    emask = jnp.concatenate(
        [jnp.broadcast_to(gmask[g:g + 1, :], (per, tm)) for g in range(N_EXPERT_GROUPS)], axis=0)
    cand = jnp.where(emask, sel, NEG_INF)

    iota_e = lax.broadcasted_iota(jnp.int32, (n_exp, tm), 0).astype(F32)
    idx_rows, w_rows = [], []
    for _ in range(TOP_K):
        m = jnp.max(cand, axis=0, keepdims=True)
        ei = jnp.min(jnp.where(cand == m, iota_e, float(n_exp)), axis=0, keepdims=True)
        hit = iota_e == ei
        w_rows.append(jnp.sum(jnp.where(hit, scores, 0.0), axis=0, keepdims=True))
        idx_rows.append(ei)
        cand = jnp.where(hit, NEG_INF, cand)
    wk = jnp.concatenate(w_rows, axis=0)
    ew_ref[...] = wk / jnp.sum(wk, axis=0, keepdims=True) * ROUTED_SCALE
    ei_ref[...] = jnp.concatenate(idx_rows, axis=0).astype(jnp.int32)
    chosen = jnp.where(emask & (cand == NEG_INF), 1.0, 0.0).astype(BF16)
    cnt_ref[...] = jnp.dot(chosen, jnp.ones((tm, LANES), BF16), preferred_element_type=F32)


def _ffn_pre(x1, mod, wgs, wus, wds, wrt, rbias, wg_cast, wu_cast, seq):
    n_tok, d = x1.shape
    tm = min(TOK_TILE, seq)
    n_steps = n_tok // tm
    assert seq % tm == 0 and n_tok % tm == 0 and d == SUBLANES * LANES
    assert wg_cast.shape == wu_cast.shape and wg_cast.shape[0] % n_steps == 0
    cast_blk = (wg_cast.shape[0] // n_steps,) + wg_cast.shape[1:]
    cast_spec = pl.BlockSpec(cast_blk, lambda i: (i, 0, 0))
    n_exp = wrt.shape[0]
    const = lambda i: (0, 0)
    return pl.pallas_call(
        _ffn_pre_kernel,
        grid=(n_tok // tm,),
        in_specs=[
            pl.BlockSpec((tm, d), lambda i: (i, 0)),
            pl.BlockSpec((None, 6, d), lambda i: ((i * tm) // seq, 0, 0)),
            pl.BlockSpec(wgs.shape, const),
            pl.BlockSpec(wus.shape, const),
            pl.BlockSpec(wds.shape, const),
            pl.BlockSpec(wrt.shape, const),
            pl.BlockSpec(rbias.shape, const),
            cast_spec,
            cast_spec,
        ],
        out_specs=[
            pl.BlockSpec((tm * SUBLANES, LANES), lambda i: (i, 0)),
            pl.BlockSpec((tm, d), lambda i: (i, 0)),
            pl.BlockSpec((TOP_K, tm), lambda i: (0, i)),
            pl.BlockSpec((TOP_K, tm), lambda i: (0, i)),
            pl.BlockSpec((None, n_exp, LANES), lambda i: (i, 0, 0)),
            cast_spec,
            cast_spec,
        ],
        out_shape=[
            jax.ShapeDtypeStruct((n_tok * SUBLANES, LANES), F32),
            jax.ShapeDtypeStruct((n_tok, d), F32),
            jax.ShapeDtypeStruct((TOP_K, n_tok), jnp.int32),
            jax.ShapeDtypeStruct((TOP_K, n_tok), F32),
            jax.ShapeDtypeStruct((n_tok // tm, n_exp, LANES), F32),
            jax.ShapeDtypeStruct(wg_cast.shape, BF16),
            jax.ShapeDtypeStruct(wu_cast.shape, BF16),
        ],
        compiler_params=pltpu.CompilerParams(
            dimension_semantics=("arbitrary",), vmem_limit_bytes=VMEM_LIMIT),
        name="ffn_pre",
    )(x1, mod, wgs, wus, wds, wrt, rbias, wg_cast, wu_cast)


def _moe_kernel(off_ref, cnt_ref, tok_ref, w_ref, h_hbm, wg_ref, wu_ref, wd_ref, out_hbm,
                h_vmem, acc_ref, xa_ref, xb_ref, xd_ref, ya_ref, yb_ref, yd_ref, sem, *, tt):
    j = pl.program_id(0)
    i = pl.program_id(1)
    n_grp = pl.num_programs(1)
    n_exp = n_grp * MOE_EXPERTS_PER_STEP
    rows = xa_ref.shape[0] // SUBLANES
    last = pl.num_programs(0) * n_exp - 1
    row_iota = lax.broadcasted_iota(jnp.int32, (rows, 1), 0)

    def slab(r):
        if isinstance(r, int):
            return pl.ds(r * SUBLANES, SUBLANES)
        return pl.ds(pl.multiple_of(r * SUBLANES, SUBLANES), SUBLANES)

    def gather_rows(x_ref, base, row_ids):
        for r in row_ids:
            x_ref[slab(r), :] = h_vmem[pl.ds(pl.multiple_of(tok_ref[0, 0, base + r], SUBLANES), SUBLANES), :]

    def scatter_rows(y_ref, base, row_ids):
        for g in range(0, len(row_ids), RMW_UNROLL):
            updates = []
            for r in row_ids[g:g + RMW_UNROLL]:
                dst = pl.ds(pl.multiple_of(tok_ref[0, 0, base + r], SUBLANES), SUBLANES)
                updates.append((dst, acc_ref[dst, :] + w_ref[0, 0, base + r] * y_ref[slab(r), :]))
            for dst, val in reversed(updates):
                acc_ref[dst, :] = val

    def expert_mlp(x_ref, y_ref, k, n_valid):
        x = _from_slab(x_ref, rows).astype(BF16)
        gate = jnp.dot(x, wg_ref[k], preferred_element_type=F32)
        up = jnp.dot(x, wu_ref[k], preferred_element_type=F32)
        act = (gate * jax.nn.sigmoid(gate) * up).astype(BF16)
        y = jnp.dot(act, wd_ref[k], preferred_element_type=F32)
        _to_slab(y_ref, jnp.where(row_iota < n_valid, y, 0.0))

    def gather_loop(x_ref, base):
        def body(g, carry):
            gather_rows(x_ref, base, [g * SUBLANES + u for u in range(SUBLANES)])
            return carry
        lax.fori_loop(0, rows // SUBLANES, body, 0)

    def scatter_loop(y_ref, base):
        def body(g, carry):
            scatter_rows(y_ref, base, [g * RMW_UNROLL + u for u in range(RMW_UNROLL)])
            return carry
        lax.fori_loop(0, rows // RMW_UNROLL, body, 0)

    idx0 = j * n_exp + i * MOE_EXPERTS_PER_STEP

    @pl.when(i == 0)
    def _():
        load = pltpu.make_async_copy(h_hbm.at[pl.ds(j * (tt * SUBLANES), tt * SUBLANES)], h_vmem, sem)
        load.start()
        acc_ref[...] = jnp.zeros_like(acc_ref)
        ya_ref[...] = jnp.zeros_like(ya_ref)
        yb_ref[...] = jnp.zeros_like(yb_ref)
        load.wait()
        gather_loop(xa_ref, off_ref[idx0])

    static_rows = list(range(rows))
    bufs = ((xa_ref, xb_ref, ya_ref, yb_ref), (xb_ref, xa_ref, yb_ref, ya_ref))
    for k in range(MOE_EXPERTS_PER_STEP):
        x_cur, x_next, y_cur, y_prev = bufs[k % 2]
        idx = idx0 + k
        scatter_rows(y_prev, off_ref[jnp.maximum(idx - 1, 0)], static_rows)
        expert_mlp(x_cur, y_cur, k, cnt_ref[idx])
        gather_rows(x_next, off_ref[jnp.minimum(idx + 1, last)], static_rows)

    for k in range(MOE_EXPERTS_PER_STEP):
        off = off_ref[idx0 + k]
        cnt = cnt_ref[idx0 + k]

        def extra(c, carry):
            gather_loop(xd_ref, off + c * rows)
            expert_mlp(xd_ref, yd_ref, k, cnt - c * rows)
            scatter_loop(yd_ref, off + c * rows)
            return carry

        lax.fori_loop(1, (cnt + rows - 1) // rows, extra, 0)

    @pl.when(i == n_grp - 1)
    def _():
        scatter_loop(bufs[(MOE_EXPERTS_PER_STEP - 1) % 2][2], off_ref[idx0 + MOE_EXPERTS_PER_STEP - 1])
        store = pltpu.make_async_copy(acc_ref, out_hbm.at[pl.ds(j * (tt * SUBLANES), tt * SUBLANES)], sem)
        store.start()
        store.wait()


def _moe(h_slab, tok_s, w_s, off, cnt, w_gate, w_up, w_down, tt):
    n_tok = h_slab.shape[0] // SUBLANES
    n_exp, d, f = w_gate.shape
    n_super = n_tok // tt
    rows = MOE_ROWS
    eb = MOE_EXPERTS_PER_STEP
    assert rows % (2 * SUBLANES) == 0 and rows % RMW_UNROLL == 0 and n_exp % eb == 0 and eb % 2 == 0
    assert tok_s.shape[2] >= tt * TOP_K + rows
    w_map = lambda j, i, off, cnt: (i, 0, 0)
    lst_map = lambda j, i, off, cnt: (j, 0, 0)
    return pl.pallas_call(
        functools.partial(_moe_kernel, tt=tt),
        grid_spec=pltpu.PrefetchScalarGridSpec(
            num_scalar_prefetch=2,
            grid=(n_super, n_exp // eb),
            in_specs=[
                pl.BlockSpec((1, 1, tok_s.shape[2]), lst_map, memory_space=pltpu.SMEM),
                pl.BlockSpec((1, 1, w_s.shape[2]), lst_map, memory_space=pltpu.SMEM),
                pl.BlockSpec(memory_space=pl.ANY),
                pl.BlockSpec((eb, d, f), w_map),
                pl.BlockSpec((eb, d, f), w_map),
                pl.BlockSpec((eb, f, d), w_map),
            ],
            out_specs=pl.BlockSpec(memory_space=pl.ANY),
            scratch_shapes=[
                pltpu.VMEM((tt * SUBLANES, LANES), F32),
                pltpu.VMEM((tt * SUBLANES, LANES), F32),
                *[pltpu.VMEM((rows * SUBLANES, LANES), F32) for _ in range(6)],
                pltpu.SemaphoreType.DMA,
            ],
        ),
        out_shape=jax.ShapeDtypeStruct((n_tok * SUBLANES, LANES), F32),
        compiler_params=pltpu.CompilerParams(
            dimension_semantics=("arbitrary", "arbitrary"), vmem_limit_bytes=VMEM_LIMIT),
        name="moe",
    )(off, cnt, tok_s, w_s, h_slab, w_gate, w_up, w_down)


def _dispatch_lists(eidx_t, ew_t, counts, tt):
    n_tok = eidx_t.shape[1]
    n_tiles, n_exp = counts.shape[0], counts.shape[1]
    n_super = n_tok // tt
    t = jnp.arange(n_tok, dtype=jnp.int32)
    key = ((t // tt) * n_exp + eidx_t) * tt + t % tt
    key_s, w_s = lax.sort((key.reshape(-1), ew_t.reshape(-1)), num_keys=1)
    tok_s = (key_s % tt) * SUBLANES
    pad = ((0, 0), (0, 0), (0, MOE_LIST_PAD))
    tok_s = jnp.pad(tok_s.reshape(n_super, 1, tt * TOP_K), pad)
    w_s = jnp.pad(w_s.reshape(n_super, 1, tt * TOP_K), pad)
    cnt = counts[:, :, 0].astype(jnp.int32).reshape(n_super, n_tiles // n_super, n_exp).sum(axis=1)
    off = jnp.cumsum(cnt, axis=1) - cnt
    return tok_s, w_s, off.reshape(-1), cnt.reshape(-1)


def _final_kernel(x_ref, sh_ref, ffn_ref, mod_ref, ln_ref, o_ref, *, alpha):
    tm = x_ref.shape[0]
    ffn = sh_ref[...] + _from_slab(ffn_ref, tm)
    z = alpha * x_ref[...] + (1.0 + mod_ref[5:6, :]) * ffn
    o_ref[...] = _layer_norm_rows(z, ln_ref[0:1, :], ln_ref[1:2, :])


def _final(x1, shared, routed_slab, mod, ln, seq, alpha):
    n_tok, d = x1.shape
    tm = min(FIN_TILE, seq)
    assert seq % tm == 0
    return pl.pallas_call(
        functools.partial(_final_kernel, alpha=alpha),
        grid=(n_tok // tm,),
        in_specs=[
            pl.BlockSpec((tm, d), lambda i: (i, 0)),
            pl.BlockSpec((tm, d), lambda i: (i, 0)),
            pl.BlockSpec((tm * SUBLANES, LANES), lambda i: (i, 0)),
            pl.BlockSpec((None, 6, d), lambda i: ((i * tm) // seq, 0, 0)),
            pl.BlockSpec(ln.shape, lambda i: (0, 0)),
        ],
        out_specs=pl.BlockSpec((tm, d), lambda i: (i, 0)),
        out_shape=jax.ShapeDtypeStruct((n_tok, d), F32),
        compiler_params=pltpu.CompilerParams(
            dimension_semantics=("arbitrary",), vmem_limit_bytes=VMEM_LIMIT),
        name="final",
    )(x1, shared, routed_slab, mod, ln)


def kernel(x, c, w_ada, b_ada, w_in, sinks, conv_w, conv_b, conv_ln_g, conv_ln_b, w_o, ln1_g, ln1_b,
           w_router, router_bias, w_gate_e, w_up_e, w_down_e, w_gate_s, w_up_s, w_down_s, ln2_g, ln2_b):
    bsz, seq, d = x.shape
    depth = w_ada.shape[0]
    n_exp = w_router.shape[2]
    n_tok = bsz * seq
    alpha = (2.0 * depth) ** 0.25
    tt = min(SUPER_TILE, n_tok)
    assert n_tok % tt == 0 and tt % min(TOK_TILE, seq) == 0

    for l in range(depth):
        mod = _ada(c, w_ada[l], b_ada[l]).reshape(bsz, 6, d)
        conv_p = jnp.stack([conv_b[l], conv_ln_g[l], conv_ln_b[l]])
        x1, wd_bf = _mix(x, mod, w_in[l].astype(BF16), sinks[l], conv_w[l], conv_p, w_o[l].astype(BF16),
                         jnp.stack([ln1_g[l], ln1_b[l]]), w_down_e[l], alpha)
        x1 = x1.reshape(n_tok, d)
        h_slab, shared, eidx_t, ew_t, counts, wg_bf, wu_bf = _ffn_pre(
            x1, mod, w_gate_s[l].astype(BF16), w_up_s[l].astype(BF16), w_down_s[l].astype(BF16),
            w_router[l].T.astype(BF16), router_bias[l].reshape(n_exp, 1), w_gate_e[l], w_up_e[l], seq)
        tok_s, w_s, off, cnt = _dispatch_lists(eidx_t, ew_t, counts, tt)
        routed = _moe(h_slab, tok_s, w_s, off, cnt, wg_bf, wu_bf, wd_bf, tt)
        x = _final(x1, shared, routed, mod, jnp.stack([ln2_g[l], ln2_b[l]]), seq, alpha)
        x = x.reshape(bsz, seq, d)
    return x
```

```python
import functools
import math

import jax
import jax.numpy as jnp
from jax import lax
from jax.experimental import pallas as pl
from jax.experimental.pallas import tpu as pltpu

F32 = jnp.float32
BF16 = jnp.bfloat16
NEG_INF = float("-inf")

HEAD_DIM = 64
WINDOW = 128
CONV_KERNEL = 31
CONV_HIST = 32
TOP_K = 8
N_EXPERT_GROUPS = 8
TOPK_EXPERT_GROUPS = 4
ROUTED_SCALE = 2.5
LN_EPS = 1e-5

LANES = 128
SUBLANES = 8
SEQ_TILE = 512
TOK_TILE = 512
FIN_TILE = 512
SUPER_TILE = 4096
MOE_ROWS = 192
MOE_EXPERTS_PER_STEP = 4
MOE_LIST_PAD = 256
RMW_UNROLL = 8
CONV_ROWS = 64
VMEM_LIMIT = 56 * 1024 * 1024


def _alibi_slopes(n_heads):
    return [2.0 ** (-8.0 * (i + 1) / n_heads) for i in range(n_heads)]


def _layer_norm_rows(z, g, b):
    mu = jnp.mean(z, axis=-1, keepdims=True)
    d = z - mu
    var = jnp.mean(d * d, axis=-1, keepdims=True)
    return d * lax.rsqrt(var + LN_EPS) * g + b


def _to_slab(ref, val):
    n = val.shape[0]
    for s in range(SUBLANES):
        ref[pl.ds(s, n, stride=SUBLANES), :] = val[:, s * LANES:(s + 1) * LANES]


def _from_slab(ref, n):
    return jnp.concatenate([ref[pl.ds(s, n, stride=SUBLANES), :] for s in range(SUBLANES)], axis=1)


def _ada_kernel(c_ref, w_ref, b_ref, o_ref):
    c = c_ref[...]
    ca = (c * jax.nn.sigmoid(c)).astype(BF16)
    o_ref[...] = jnp.dot(ca, w_ref[...].astype(BF16), preferred_element_type=F32) + b_ref[...]


def _ada(c, w_ada, b_ada):
    bsz, d = c.shape
    n_out = w_ada.shape[1]
    return pl.pallas_call(
        _ada_kernel,
        grid=(n_out // d,),
        in_specs=[
            pl.BlockSpec((bsz, d), lambda j: (0, 0)),
            pl.BlockSpec((d, d), lambda j: (0, j)),
            pl.BlockSpec((1, d), lambda j: (0, j)),
        ],
        out_specs=pl.BlockSpec((bsz, d), lambda j: (0, j)),
        out_shape=jax.ShapeDtypeStruct((bsz, n_out), F32),
        compiler_params=pltpu.CompilerParams(vmem_limit_bytes=VMEM_LIMIT),
        name="ada",
    )(c, w_ada, b_ada.reshape(1, n_out))


def _mix_kernel(sinks_ref, x_ref, mod_ref, win_ref, convw_ref, convp_ref, wo_ref, ln_ref, wca_ref, wcb_ref,
                o_ref, wca_out_ref, wcb_out_ref, q_ref, ke_ref, ve_ref, glu_ref, gsh_ref, cat_ref,
                *, ts, aw, alpha):
    wca_out_ref[...] = wca_ref[...].astype(BF16)
    wcb_out_ref[...] = wcb_ref[...].astype(BF16)
    s_idx = pl.program_id(1)
    n_heads = aw // HEAD_DIM
    slopes = _alibi_slopes(n_heads)
    cw = cat_ref.shape[1] - aw

    @pl.when(s_idx == 0)
    def _():
        ke_ref[:, 0:WINDOW, :] = jnp.zeros((4, WINDOW, LANES), BF16)
        ve_ref[:, 0:WINDOW, :] = jnp.zeros((4, WINDOW, LANES), BF16)
        glu_ref[0:CONV_HIST, :] = jnp.zeros((CONV_HIST, cw), F32)

    @pl.when(s_idx > 0)
    def _():
        ke_ref[:, 0:WINDOW, :] = ke_ref[:, ts:ts + WINDOW, :]
        ve_ref[:, 0:WINDOW, :] = ve_ref[:, ts:ts + WINDOW, :]
        glu_ref[0:CONV_HIST, :] = glu_ref[ts:ts + CONV_HIST, :]

    x = x_ref[...]
    h = (x * (1.0 + mod_ref[1:2, :]) + mod_ref[0:1, :]).astype(BF16)

    q = jnp.dot(h, win_ref[:, 0:aw], preferred_element_type=F32)
    q_ref[...] = (q * (1.0 / math.sqrt(HEAD_DIM))).astype(BF16)
    kv = jnp.dot(h, win_ref[:, aw:aw + 2 * LANES], preferred_element_type=F32)
    lo = lax.broadcasted_iota(jnp.int32, (ts, LANES), 1) < HEAD_DIM
    for dst, t in ((ke_ref, kv[:, 0:LANES]), (ve_ref, kv[:, LANES:2 * LANES])):
        t_r = pltpu.roll(t, HEAD_DIM, axis=1)
        dst[0, WINDOW:WINDOW + ts, :] = jnp.where(lo, t, 0.0).astype(BF16)
        dst[1, WINDOW:WINDOW + ts, :] = jnp.where(lo, 0.0, t_r).astype(BF16)
        dst[2, WINDOW:WINDOW + ts, :] = jnp.where(lo, t_r, 0.0).astype(BF16)
        dst[3, WINDOW:WINDOW + ts, :] = jnp.where(lo, 0.0, t).astype(BF16)
    u0 = aw + 2 * LANES
    ga = jnp.dot(h, win_ref[:, u0:u0 + cw], preferred_element_type=F32)
    gb = jnp.dot(h, win_ref[:, u0 + cw:u0 + 2 * cw], preferred_element_type=F32)
    glu_ref[CONV_HIST:CONV_HIST + ts, :] = ga * jax.nn.sigmoid(gb)

    qi = lax.broadcasted_iota(jnp.int32, (WINDOW, 2 * WINDOW), 0)
    kj = lax.broadcasted_iota(jnp.int32, (WINDOW, 2 * WINDOW), 1)
    dist = WINDOW + qi - kj
    band = (dist >= 0) & (dist < WINDOW)
    neg_dist = -dist.astype(F32)
    bias_any = jnp.where(band, neg_dist, NEG_INF)
    bias_first = jnp.where(band & ((kj >= WINDOW) | (s_idx > 0)), neg_dist, NEG_INF)
    for i in range(ts // WINDOW):
        r0 = i * WINDOW
        bias = bias_first if i == 0 else bias_any
        for pair in range(aw // LANES):
            g = (2 * pair) // (n_heads // 2)
            qp = q_ref[r0:r0 + WINDOW, pair * LANES:(pair + 1) * LANES]
            out_pair = None
            for par in range(2):
                hd = 2 * pair + par
                kk = ke_ref[2 * g + par, r0:r0 + 2 * WINDOW, :]
                s = lax.dot_general(qp, kk, (((1,), (1,)), ((), ())), preferred_element_type=F32)
                s = s + slopes[hd] * bias
                sink = sinks_ref[hd]
                m = jnp.maximum(jnp.max(s, axis=-1, keepdims=True), sink)
                p = jnp.exp(s - m)
                denom = jnp.sum(p, axis=-1, keepdims=True) + jnp.exp(sink - m)
                vv = ve_ref[2 * g + par, r0:r0 + 2 * WINDOW, :]
                o = jnp.dot(p.astype(BF16), vv, preferred_element_type=F32) * (1.0 / denom)
                out_pair = o if out_pair is None else out_pair + o
            cat_ref[r0:r0 + WINDOW, pair * LANES:(pair + 1) * LANES] = out_pair.astype(BF16)

    conv_b = convp_ref[0:1, :]
    cln_g = convp_ref[1:2, :]
    cln_b = convp_ref[2:3, :]
    off = CONV_HIST - (CONV_KERNEL - 1)
    n_sh = gsh_ref.shape[1]
    for p in range(1, SUBLANES):
        gsh_ref[p - 1] = glu_ref[p:p + n_sh, :]
    for c in range(ts // CONV_ROWS):
        c0 = c * CONV_ROWS
        acc = jnp.broadcast_to(conv_b, (CONV_ROWS, cw))
        for j in range(CONV_KERNEL):
            a, p = divmod(off + j, SUBLANES)
            r0 = c0 + a * SUBLANES
            tap = glu_ref[r0:r0 + CONV_ROWS, :] if p == 0 else gsh_ref[p - 1, r0:r0 + CONV_ROWS, :]
            acc = acc + tap * convw_ref[j:j + 1, :]
        yn = _layer_norm_rows(acc, cln_g, cln_b)
        cat_ref[c0:c0 + CONV_ROWS, aw:aw + cw] = (yn * jax.nn.sigmoid(yn)).astype(BF16)

    mix = jnp.dot(cat_ref[...], wo_ref[...], preferred_element_type=F32)
    z = alpha * x + (1.0 + mod_ref[2:3, :]) * mix
    o_ref[...] = _layer_norm_rows(z, ln_ref[0:1, :], ln_ref[1:2, :])


def _mix(x, mod, w_in, sinks, conv_w, conv_p, w_o, ln, w_cast, w_cast2, alpha):
    bsz, seq, d = x.shape
    cw = conv_w.shape[1]
    aw = d - cw
    ts = min(SEQ_TILE, seq)
    n_seq = seq // ts
    assert seq % ts == 0 and ts % WINDOW == 0 and aw % LANES == 0
    assert (aw // HEAD_DIM) // 4 == 2, "kernel packs exactly two KV heads into one lane group"
    assert w_in.shape[1] == aw + 2 * LANES + 2 * cw
    assert w_cast.shape[0] % (bsz * n_seq) == 0 and w_cast2.shape[0] == w_cast.shape[0]
    cast_map = lambda b, s: (b * n_seq + s, 0, 0)
    cast_spec = pl.BlockSpec((w_cast.shape[0] // (bsz * n_seq),) + w_cast.shape[1:], cast_map)
    cast2_spec = pl.BlockSpec((w_cast2.shape[0] // (bsz * n_seq),) + w_cast2.shape[1:], cast_map)
    kern = functools.partial(_mix_kernel, ts=ts, aw=aw, alpha=alpha)
    const = lambda b, s: (0, 0)
    resident = pl.Buffered(1)
    return pl.pallas_call(
        kern,
        grid=(bsz, seq // ts),
        in_specs=[
            pl.BlockSpec(memory_space=pltpu.SMEM),
            pl.BlockSpec((None, ts, d), lambda b, s: (b, s, 0)),
            pl.BlockSpec((None, 6, d), lambda b, s: (b, 0, 0)),
            pl.BlockSpec(w_in.shape, const, pipeline_mode=resident),
            pl.BlockSpec(conv_w.shape, const),
            pl.BlockSpec(conv_p.shape, const),
            pl.BlockSpec(w_o.shape, const, pipeline_mode=resident),
            pl.BlockSpec(ln.shape, const),
            cast_spec,
            cast2_spec,
        ],
        out_specs=[pl.BlockSpec((None, ts, d), lambda b, s: (b, s, 0)), cast_spec, cast2_spec],
        out_shape=[jax.ShapeDtypeStruct((bsz, seq, d), F32), jax.ShapeDtypeStruct(w_cast.shape, BF16),
                   jax.ShapeDtypeStruct(w_cast2.shape, BF16)],
        scratch_shapes=[
            pltpu.VMEM((ts, aw), BF16),
            pltpu.VMEM((4, ts + WINDOW, LANES), BF16),
            pltpu.VMEM((4, ts + WINDOW, LANES), BF16),
            pltpu.VMEM((ts + CONV_HIST, cw), F32),
            pltpu.VMEM((SUBLANES - 1, ts + CONV_HIST - SUBLANES, cw), F32),
            pltpu.VMEM((ts, d), BF16),
        ],
        compiler_params=pltpu.CompilerParams(
            dimension_semantics=("arbitrary", "arbitrary"), vmem_limit_bytes=VMEM_LIMIT),
        name="mix",
    )(sinks, x, mod, w_in, conv_w, conv_p, w_o, ln, w_cast, w_cast2)


def _ffn_pre_kernel(x_ref, mod_ref, wgs_ref, wus_ref, wds_ref, wrt_ref, rb_ref, wca_ref,
                    h_ref, sh_ref, ei_ref, ew_ref, cnt_ref, wca_out_ref):
    wca_out_ref[...] = wca_ref[...].astype(BF16)
    tm = x_ref.shape[0]
    n_exp = wrt_ref.shape[0]
    per = n_exp // N_EXPERT_GROUPS
    hf = x_ref[...] * (1.0 + mod_ref[4:5, :]) + mod_ref[3:4, :]
    _to_slab(h_ref, hf)
    h = hf.astype(BF16)

    gate = jnp.dot(h, wgs_ref[...], preferred_element_type=F32)
    up = jnp.dot(h, wus_ref[...], preferred_element_type=F32)
    act = (gate * jax.nn.sigmoid(gate) * up).astype(BF16)
    sh_ref[...] = jnp.dot(act, wds_ref[...], preferred_element_type=F32)

    logits = lax.dot_general(wrt_ref[...], h, (((1,), (1,)), ((), ())), preferred_element_type=F32)
    scores = jax.nn.sigmoid(logits)
    sel = scores + rb_ref[...]

    iota_p = lax.broadcasted_iota(jnp.int32, (per, tm), 0).astype(F32)
    gs_rows = []
    for g in range(N_EXPERT_GROUPS):
        blk = sel[g * per:(g + 1) * per, :]
        m1 = jnp.max(blk, axis=0, keepdims=True)
        i1 = jnp.min(jnp.where(blk == m1, iota_p, float(per)), axis=0, keepdims=True)
        m2 = jnp.max(jnp.where(iota_p == i1, NEG_INF, blk), axis=0, keepdims=True)
        gs_rows.append(m1 + m2)
    gs = jnp.concatenate(gs_rows, axis=0)
    iota_g = lax.broadcasted_iota(jnp.int32, (N_EXPERT_GROUPS, tm), 0).astype(F32)
    gmask = jnp.zeros((N_EXPERT_GROUPS, tm), jnp.bool_)
    for _ in range(TOPK_EXPERT_GROUPS):
        m = jnp.max(gs, axis=0, keepdims=True)
        gi = jnp.min(jnp.where(gs == m, iota_g, float(N_EXPERT_GROUPS)), axis=0, keepdims=True)
        hit = iota_g == gi
        gmask = gmask | hit
        gs = jnp.where(hit, NEG_INF, gs)
    emask = jnp.concatenate(
        [jnp.broadcast_to(gmask[g:g + 1, :], (per, tm)) for g in range(N_EXPERT_GROUPS)], axis=0)
    cand = jnp.where(emask, sel, NEG_INF)

    iota_e = lax.broadcasted_iota(jnp.int32, (n_exp, tm), 0).astype(F32)
    idx_rows, w_rows = [], []
    for _ in range(TOP_K):
        m = jnp.max(cand, axis=0, keepdims=True)
        ei = jnp.min(jnp.where(cand == m, iota_e, float(n_exp)), axis=0, keepdims=True)
        hit = iota_e == ei
        w_rows.append(jnp.sum(jnp.where(hit, scores, 0.0), axis=0, keepdims=True))
        idx_rows.append(ei)
        cand = jnp.where(hit, NEG_INF, cand)
    wk = jnp.concatenate(w_rows, axis=0)
    ew_ref[...] = wk / jnp.sum(wk, axis=0, keepdims=True) * ROUTED_SCALE
    ei_ref[...] = jnp.concatenate(idx_rows, axis=0).astype(jnp.int32)
    chosen = jnp.where(emask & (cand == NEG_INF), 1.0, 0.0).astype(BF16)
    cnt_ref[...] = jnp.dot(chosen, jnp.ones((tm, LANES), BF16), preferred_element_type=F32)


def _ffn_pre(x1, mod, wgs, wus, wds, wrt, rbias, w_cast, seq):
    n_tok, d = x1.shape
    tm = min(TOK_TILE, seq)
    n_steps = n_tok // tm
    assert seq % tm == 0 and n_tok % tm == 0 and d == SUBLANES * LANES
    assert w_cast.shape[0] % n_steps == 0
    cast_blk = (w_cast.shape[0] // n_steps,) + w_cast.shape[1:]
    cast_spec = pl.BlockSpec(cast_blk, lambda i: (i, 0, 0))
    n_exp = wrt.shape[0]
    const = lambda i: (0, 0)
    return pl.pallas_call(
        _ffn_pre_kernel,
        grid=(n_tok // tm,),
        in_specs=[
            pl.BlockSpec((tm, d), lambda i: (i, 0)),
            pl.BlockSpec((None, 6, d), lambda i: ((i * tm) // seq, 0, 0)),
            pl.BlockSpec(wgs.shape, const),
            pl.BlockSpec(wus.shape, const),
            pl.BlockSpec(wds.shape, const),
            pl.BlockSpec(wrt.shape, const),
            pl.BlockSpec(rbias.shape, const),
            cast_spec,
        ],
        out_specs=[
            pl.BlockSpec((tm * SUBLANES, LANES), lambda i: (i, 0)),
            pl.BlockSpec((tm, d), lambda i: (i, 0)),
            pl.BlockSpec((TOP_K, tm), lambda i: (0, i)),
            pl.BlockSpec((TOP_K, tm), lambda i: (0, i)),
            pl.BlockSpec((None, n_exp, LANES), lambda i: (i, 0, 0)),
            cast_spec,
        ],
        out_shape=[
            jax.ShapeDtypeStruct((n_tok * SUBLANES, LANES), F32),
            jax.ShapeDtypeStruct((n_tok, d), F32),
            jax.ShapeDtypeStruct((TOP_K, n_tok), jnp.int32),
            jax.ShapeDtypeStruct((TOP_K, n_tok), F32),
            jax.ShapeDtypeStruct((n_tok // tm, n_exp, LANES), F32),
            jax.ShapeDtypeStruct(w_cast.shape, BF16),
        ],
        compiler_params=pltpu.CompilerParams(
            dimension_semantics=("arbitrary",), vmem_limit_bytes=VMEM_LIMIT),
        name="ffn_pre",
    )(x1, mod, wgs, wus, wds, wrt, rbias, w_cast)


def _moe_kernel(off_ref, cnt_ref, tok_ref, w_ref, h_hbm, wg_ref, wu_ref, wd_ref, out_hbm,
                h_vmem, acc_ref, xa_ref, xb_ref, xd_ref, ya_ref, yb_ref, yd_ref, sem, *, tt):
    j = pl.program_id(0)
    i = pl.program_id(1)
    n_grp = pl.num_programs(1)
    n_exp = n_grp * MOE_EXPERTS_PER_STEP
    rows = xa_ref.shape[0] // SUBLANES
    last = pl.num_programs(0) * n_exp - 1
    row_iota = lax.broadcasted_iota(jnp.int32, (rows, 1), 0)

    def slab(r):
        if isinstance(r, int):
            return pl.ds(r * SUBLANES, SUBLANES)
        return pl.ds(pl.multiple_of(r * SUBLANES, SUBLANES), SUBLANES)

    def gather_rows(x_ref, base, row_ids):
        for r in row_ids:
            x_ref[slab(r), :] = h_vmem[pl.ds(pl.multiple_of(tok_ref[0, 0, base + r], SUBLANES), SUBLANES), :]

    def scatter_rows(y_ref, base, row_ids):
        for g in range(0, len(row_ids), RMW_UNROLL):
            updates = []
            for r in row_ids[g:g + RMW_UNROLL]:
                dst = pl.ds(pl.multiple_of(tok_ref[0, 0, base + r], SUBLANES), SUBLANES)
                updates.append((dst, acc_ref[dst, :] + w_ref[0, 0, base + r] * y_ref[slab(r), :]))
            for dst, val in reversed(updates):
                acc_ref[dst, :] = val

    def expert_mlp(x_ref, y_ref, k, n_valid):
        x = _from_slab(x_ref, rows).astype(BF16)
        gate = jnp.dot(x, wg_ref[k], preferred_element_type=F32)
        up = jnp.dot(x, wu_ref[k], preferred_element_type=F32)
        act = (gate * jax.nn.sigmoid(gate) * up).astype(BF16)
        y = jnp.dot(act, wd_ref[k], preferred_element_type=F32)
        _to_slab(y_ref, jnp.where(row_iota < n_valid, y, 0.0))

    def gather_loop(x_ref, base):
        def body(g, carry):
            gather_rows(x_ref, base, [g * SUBLANES + u for u in range(SUBLANES)])
            return carry
        lax.fori_loop(0, rows // SUBLANES, body, 0)

    def scatter_loop(y_ref, base, n_rows):
        def body(g, carry):
            scatter_rows(y_ref, base, [g * RMW_UNROLL + u for u in range(RMW_UNROLL)])
            return carry
        lax.fori_loop(0, (jnp.minimum(n_rows, rows) + RMW_UNROLL - 1) // RMW_UNROLL, body, 0)

    idx0 = j * n_exp + i * MOE_EXPERTS_PER_STEP

    @pl.when(i == 0)
    def _():
        load = pltpu.make_async_copy(h_hbm.at[pl.ds(j * (tt * SUBLANES), tt * SUBLANES)], h_vmem, sem)
        load.start()
        acc_ref[...] = jnp.zeros_like(acc_ref)
        load.wait()
        gather_loop(xa_ref, off_ref[idx0])

    static_rows = list(range(rows))
    bufs = ((xa_ref, xb_ref, ya_ref, yb_ref), (xb_ref, xa_ref, yb_ref, ya_ref))
    for k in range(MOE_EXPERTS_PER_STEP):
        x_cur, x_next, y_cur, y_prev = bufs[k % 2]
        idx = idx0 + k
        idx_p = jnp.maximum(idx - 1, 0)
        n_prev = cnt_ref[idx_p] if k > 0 else jnp.where(i > 0, cnt_ref[idx_p], 0)
        scatter_loop(y_prev, off_ref[idx_p], n_prev)
        expert_mlp(x_cur, y_cur, k, cnt_ref[idx])
        gather_rows(x_next, off_ref[jnp.minimum(idx + 1, last)], static_rows)

    for k in range(MOE_EXPERTS_PER_STEP):
        off = off_ref[idx0 + k]
        cnt = cnt_ref[idx0 + k]

        def extra(c, carry):
            gather_loop(xd_ref, off + c * rows)
            expert_mlp(xd_ref, yd_ref, k, cnt - c * rows)
            scatter_loop(yd_ref, off + c * rows, cnt - c * rows)
            return carry

        lax.fori_loop(1, (cnt + rows - 1) // rows, extra, 0)

    @pl.when(i == n_grp - 1)
    def _():
        idx_l = idx0 + MOE_EXPERTS_PER_STEP - 1
        scatter_loop(bufs[(MOE_EXPERTS_PER_STEP - 1) % 2][2], off_ref[idx_l], cnt_ref[idx_l])
        store = pltpu.make_async_copy(acc_ref, out_hbm.at[pl.ds(j * (tt * SUBLANES), tt * SUBLANES)], sem)
        store.start()
        store.wait()


def _moe(h_slab, tok_s, w_s, off, cnt, w_gate, w_up, w_down, tt):
    n_tok = h_slab.shape[0] // SUBLANES
    n_exp, d, f = w_gate.shape
    n_super = n_tok // tt
    rows = MOE_ROWS
    eb = MOE_EXPERTS_PER_STEP
    assert rows % (2 * SUBLANES) == 0 and rows % RMW_UNROLL == 0 and n_exp % eb == 0 and eb % 2 == 0
    assert tok_s.shape[2] >= tt * TOP_K + rows
    w_map = lambda j, i, off, cnt: (i, 0, 0)
    lst_map = lambda j, i, off, cnt: (j, 0, 0)
    return pl.pallas_call(
        functools.partial(_moe_kernel, tt=tt),
        grid_spec=pltpu.PrefetchScalarGridSpec(
            num_scalar_prefetch=2,
            grid=(n_super, n_exp // eb),
            in_specs=[
                pl.BlockSpec((1, 1, tok_s.shape[2]), lst_map, memory_space=pltpu.SMEM),
                pl.BlockSpec((1, 1, w_s.shape[2]), lst_map, memory_space=pltpu.SMEM),
                pl.BlockSpec(memory_space=pl.ANY),
                pl.BlockSpec((eb, d, f), w_map),
                pl.BlockSpec((eb, d, f), w_map),
                pl.BlockSpec((eb, f, d), w_map),
            ],
            out_specs=pl.BlockSpec(memory_space=pl.ANY),
            scratch_shapes=[
                pltpu.VMEM((tt * SUBLANES, LANES), F32),
                pltpu.VMEM((tt * SUBLANES, LANES), F32),
                *[pltpu.VMEM((rows * SUBLANES, LANES), F32) for _ in range(6)],
                pltpu.SemaphoreType.DMA,
            ],
        ),
        out_shape=jax.ShapeDtypeStruct((n_tok * SUBLANES, LANES), F32),
        compiler_params=pltpu.CompilerParams(
            dimension_semantics=("arbitrary", "arbitrary"), vmem_limit_bytes=VMEM_LIMIT),
        name="moe",
    )(off, cnt, tok_s, w_s, h_slab, w_gate, w_up, w_down)


def _dispatch_lists(eidx_t, ew_t, counts, tt):
    n_tok = eidx_t.shape[1]
    n_tiles, n_exp = counts.shape[0], counts.shape[1]
    n_super = n_tok // tt
    key = eidx_t * tt + jnp.arange(n_tok, dtype=jnp.int32) % tt

    def by_super_tile(a):
        return a.reshape(TOP_K, n_super, tt).transpose(1, 0, 2).reshape(n_super, TOP_K * tt)

    key_s, w_s = lax.sort((by_super_tile(key), by_super_tile(ew_t)), dimension=1, num_keys=1)
    tok_s = (key_s % tt) * SUBLANES
    pad = ((0, 0), (0, 0), (0, MOE_LIST_PAD))
    tok_s = jnp.pad(tok_s.reshape(n_super, 1, tt * TOP_K), pad)
    w_s = jnp.pad(w_s.reshape(n_super, 1, tt * TOP_K), pad)
    cnt = counts[:, :, 0].astype(jnp.int32).reshape(n_super, n_tiles // n_super, n_exp).sum(axis=1)
    off = jnp.cumsum(cnt, axis=1) - cnt
    return tok_s, w_s, off.reshape(-1), cnt.reshape(-1)


def _final_kernel(x_ref, sh_ref, ffn_ref, mod_ref, ln_ref, o_ref, *, alpha):
    tm = x_ref.shape[0]
    ffn = sh_ref[...] + _from_slab(ffn_ref, tm)
    z = alpha * x_ref[...] + (1.0 + mod_ref[5:6, :]) * ffn
    o_ref[...] = _layer_norm_rows(z, ln_ref[0:1, :], ln_ref[1:2, :])


def _final(x1, shared, routed_slab, mod, ln, seq, alpha):
    n_tok, d = x1.shape
    tm = min(FIN_TILE, seq)
    assert seq % tm == 0
    return pl.pallas_call(
        functools.partial(_final_kernel, alpha=alpha),
        grid=(n_tok // tm,),
        in_specs=[
            pl.BlockSpec((tm, d), lambda i: (i, 0)),
            pl.BlockSpec((tm, d), lambda i: (i, 0)),
            pl.BlockSpec((tm * SUBLANES, LANES), lambda i: (i, 0)),
            pl.BlockSpec((None, 6, d), lambda i: ((i * tm) // seq, 0, 0)),
            pl.BlockSpec(ln.shape, lambda i: (0, 0)),
        ],
        out_specs=pl.BlockSpec((tm, d), lambda i: (i, 0)),
        out_shape=jax.ShapeDtypeStruct((n_tok, d), F32),
        compiler_params=pltpu.CompilerParams(
            dimension_semantics=("arbitrary",), vmem_limit_bytes=VMEM_LIMIT),
        name="final",
    )(x1, shared, routed_slab, mod, ln)


def kernel(x, c, w_ada, b_ada, w_in, sinks, conv_w, conv_b, conv_ln_g, conv_ln_b, w_o, ln1_g, ln1_b,
           w_router, router_bias, w_gate_e, w_up_e, w_down_e, w_gate_s, w_up_s, w_down_s, ln2_g, ln2_b):
    bsz, seq, d = x.shape
    depth = w_ada.shape[0]
    n_exp = w_router.shape[2]
    n_tok = bsz * seq
    alpha = (2.0 * depth) ** 0.25
    tt = min(SUPER_TILE, n_tok)
    assert n_tok % tt == 0 and tt % min(TOK_TILE, seq) == 0

    for l in range(depth):
        mod = _ada(c, w_ada[l], b_ada[l]).reshape(bsz, 6, d)
        conv_p = jnp.stack([conv_b[l], conv_ln_g[l], conv_ln_b[l]])
        x1, wd_bf, wu_bf = _mix(x, mod, w_in[l].astype(BF16), sinks[l], conv_w[l], conv_p,
                                w_o[l].astype(BF16), jnp.stack([ln1_g[l], ln1_b[l]]),
                                w_down_e[l], w_up_e[l], alpha)
        x1 = x1.reshape(n_tok, d)
        h_slab, shared, eidx_t, ew_t, counts, wg_bf = _ffn_pre(
            x1, mod, w_gate_s[l].astype(BF16), w_up_s[l].astype(BF16), w_down_s[l].astype(BF16),
            w_router[l].T.astype(BF16), router_bias[l].reshape(n_exp, 1), w_gate_e[l], seq)
        tok_s, w_s, off, cnt = _dispatch_lists(eidx_t, ew_t, counts, tt)
        routed = _moe(h_slab, tok_s, w_s, off, cnt, wg_bf, wu_bf, wd_bf, tt)
        x = _final(x1, shared, routed, mod, jnp.stack([ln2_g[l], ln2_b[l]]), seq, alpha)
        x = x.reshape(bsz, seq, d)
    return x
```

```python
import functools
import math

import jax
import jax.numpy as jnp
from jax import lax
from jax.experimental import pallas as pl
from jax.experimental.pallas import tpu as pltpu

F32 = jnp.float32
BF16 = jnp.bfloat16
NEG_INF = float("-inf")

HEAD_DIM = 64
WINDOW = 128
CONV_KERNEL = 31
CONV_HIST = 32
TOP_K = 8
N_EXPERT_GROUPS = 8
TOPK_EXPERT_GROUPS = 4
ROUTED_SCALE = 2.5
LN_EPS = 1e-5

LANES = 128
SUBLANES = 8
SEQ_TILE = 512
TOK_TILE = 512
FIN_TILE = 512
SUPER_TILE = 4096
MOE_ROWS = 256
MOE_SEG = 168
MOE_EXPERTS_PER_STEP = 4
MOE_LIST_PAD = 256
RMW_UNROLL = 8
CONV_ROWS = 64
VMEM_LIMIT = 56 * 1024 * 1024


def _alibi_slopes(n_heads):
    return [2.0 ** (-8.0 * (i + 1) / n_heads) for i in range(n_heads)]


def _layer_norm_rows(z, g, b):
    mu = jnp.mean(z, axis=-1, keepdims=True)
    d = z - mu
    var = jnp.mean(d * d, axis=-1, keepdims=True)
    return d * lax.rsqrt(var + LN_EPS) * g + b


def _to_slab(ref, val):
    n = val.shape[0]
    for s in range(SUBLANES):
        ref[pl.ds(s, n, stride=SUBLANES), :] = val[:, s * LANES:(s + 1) * LANES]


def _from_slab(ref, n):
    return jnp.concatenate([ref[pl.ds(s, n, stride=SUBLANES), :] for s in range(SUBLANES)], axis=1)


def _ada_kernel(c_ref, w_ref, b_ref, o_ref):
    c = c_ref[...]
    ca = (c * jax.nn.sigmoid(c)).astype(BF16)
    o_ref[...] = jnp.dot(ca, w_ref[...].astype(BF16), preferred_element_type=F32) + b_ref[...]


def _ada(c, w_ada, b_ada):
    bsz, d = c.shape
    n_out = w_ada.shape[1]
    return pl.pallas_call(
        _ada_kernel,
        grid=(n_out // d,),
        in_specs=[
            pl.BlockSpec((bsz, d), lambda j: (0, 0)),
            pl.BlockSpec((d, d), lambda j: (0, j)),
            pl.BlockSpec((1, d), lambda j: (0, j)),
        ],
        out_specs=pl.BlockSpec((bsz, d), lambda j: (0, j)),
        out_shape=jax.ShapeDtypeStruct((bsz, n_out), F32),
        compiler_params=pltpu.CompilerParams(vmem_limit_bytes=VMEM_LIMIT),
        name="ada",
    )(c, w_ada, b_ada.reshape(1, n_out))


def _mix_kernel(sinks_ref, x_ref, mod_ref, win_ref, convw_ref, convp_ref, wo_ref, ln_ref, wca_ref, wcb_ref,
                o_ref, wca_out_ref, wcb_out_ref, q_ref, ke_ref, ve_ref, glu_ref, gsh_ref, cat_ref,
                *, ts, aw, alpha):
    wca_out_ref[...] = wca_ref[...].astype(BF16)
    wcb_out_ref[...] = wcb_ref[...].astype(BF16)
    s_idx = pl.program_id(1)
    n_heads = aw // HEAD_DIM
    slopes = _alibi_slopes(n_heads)
    cw = cat_ref.shape[1] - aw

    @pl.when(s_idx == 0)
    def _():
        ke_ref[:, 0:WINDOW, :] = jnp.zeros((4, WINDOW, LANES), BF16)
        ve_ref[:, 0:WINDOW, :] = jnp.zeros((4, WINDOW, LANES), BF16)
        glu_ref[0:CONV_HIST, :] = jnp.zeros((CONV_HIST, cw), F32)

    @pl.when(s_idx > 0)
    def _():
        ke_ref[:, 0:WINDOW, :] = ke_ref[:, ts:ts + WINDOW, :]
        ve_ref[:, 0:WINDOW, :] = ve_ref[:, ts:ts + WINDOW, :]
        glu_ref[0:CONV_HIST, :] = glu_ref[ts:ts + CONV_HIST, :]

    x = x_ref[...]
    h = (x * (1.0 + mod_ref[1:2, :]) + mod_ref[0:1, :]).astype(BF16)

    q = jnp.dot(h, win_ref[:, 0:aw], preferred_element_type=F32)
    q_ref[...] = (q * (1.0 / math.sqrt(HEAD_DIM))).astype(BF16)
    kv = jnp.dot(h, win_ref[:, aw:aw + 2 * LANES], preferred_element_type=F32)
    lo = lax.broadcasted_iota(jnp.int32, (ts, LANES), 1) < HEAD_DIM
    for dst, t in ((ke_ref, kv[:, 0:LANES]), (ve_ref, kv[:, LANES:2 * LANES])):
        t_r = pltpu.roll(t, HEAD_DIM, axis=1)
        dst[0, WINDOW:WINDOW + ts, :] = jnp.where(lo, t, 0.0).astype(BF16)
        dst[1, WINDOW:WINDOW + ts, :] = jnp.where(lo, 0.0, t_r).astype(BF16)
        dst[2, WINDOW:WINDOW + ts, :] = jnp.where(lo, t_r, 0.0).astype(BF16)
        dst[3, WINDOW:WINDOW + ts, :] = jnp.where(lo, 0.0, t).astype(BF16)
    u0 = aw + 2 * LANES
    ga = jnp.dot(h, win_ref[:, u0:u0 + cw], preferred_element_type=F32)
    gb = jnp.dot(h, win_ref[:, u0 + cw:u0 + 2 * cw], preferred_element_type=F32)
    glu_ref[CONV_HIST:CONV_HIST + ts, :] = ga * jax.nn.sigmoid(gb)

    qi = lax.broadcasted_iota(jnp.int32, (WINDOW, 2 * WINDOW), 0)
    kj = lax.broadcasted_iota(jnp.int32, (WINDOW, 2 * WINDOW), 1)
    dist = WINDOW + qi - kj
    band = (dist >= 0) & (dist < WINDOW)
    neg_dist = -dist.astype(F32)
    bias_any = jnp.where(band, neg_dist, NEG_INF)
    bias_first = jnp.where(band & ((kj >= WINDOW) | (s_idx > 0)), neg_dist, NEG_INF)
    for i in range(ts // WINDOW):
        r0 = i * WINDOW
        bias = bias_first if i == 0 else bias_any
        for pair in range(aw // LANES):
            g = (2 * pair) // (n_heads // 2)
            qp = q_ref[r0:r0 + WINDOW, pair * LANES:(pair + 1) * LANES]
            out_pair = None
            for par in range(2):
                hd = 2 * pair + par
                kk = ke_ref[2 * g + par, r0:r0 + 2 * WINDOW, :]
                s = lax.dot_general(qp, kk, (((1,), (1,)), ((), ())), preferred_element_type=F32)
                s = s + slopes[hd] * bias
                sink = sinks_ref[hd]
                m = jnp.maximum(jnp.max(s, axis=-1, keepdims=True), sink)
                p = jnp.exp(s - m)
                denom = jnp.sum(p, axis=-1, keepdims=True) + jnp.exp(sink - m)
                vv = ve_ref[2 * g + par, r0:r0 + 2 * WINDOW, :]
                o = jnp.dot(p.astype(BF16), vv, preferred_element_type=F32) * (1.0 / denom)
                out_pair = o if out_pair is None else out_pair + o
            cat_ref[r0:r0 + WINDOW, pair * LANES:(pair + 1) * LANES] = out_pair.astype(BF16)

    conv_b = convp_ref[0:1, :]
    cln_g = convp_ref[1:2, :]
    cln_b = convp_ref[2:3, :]
    off = CONV_HIST - (CONV_KERNEL - 1)
    n_sh = gsh_ref.shape[1]
    for p in range(1, SUBLANES):
        gsh_ref[p - 1] = glu_ref[p:p + n_sh, :]
    for c in range(ts // CONV_ROWS):
        c0 = c * CONV_ROWS
        acc = jnp.broadcast_to(conv_b, (CONV_ROWS, cw))
        for j in range(CONV_KERNEL):
            a, p = divmod(off + j, SUBLANES)
            r0 = c0 + a * SUBLANES
            tap = glu_ref[r0:r0 + CONV_ROWS, :] if p == 0 else gsh_ref[p - 1, r0:r0 + CONV_ROWS, :]
            acc = acc + tap * convw_ref[j:j + 1, :]
        yn = _layer_norm_rows(acc, cln_g, cln_b)
        cat_ref[c0:c0 + CONV_ROWS, aw:aw + cw] = (yn * jax.nn.sigmoid(yn)).astype(BF16)

    mix = jnp.dot(cat_ref[...], wo_ref[...], preferred_element_type=F32)
    z = alpha * x + (1.0 + mod_ref[2:3, :]) * mix
    o_ref[...] = _layer_norm_rows(z, ln_ref[0:1, :], ln_ref[1:2, :])


def _mix(x, mod, w_in, sinks, conv_w, conv_p, w_o, ln, w_cast, w_cast2, alpha):
    bsz, seq, d = x.shape
    cw = conv_w.shape[1]
    aw = d - cw
    ts = min(SEQ_TILE, seq)
    n_seq = seq // ts
    assert seq % ts == 0 and ts % WINDOW == 0 and aw % LANES == 0
    assert (aw // HEAD_DIM) // 4 == 2, "kernel packs exactly two KV heads into one lane group"
    assert w_in.shape[1] == aw + 2 * LANES + 2 * cw
    assert w_cast.shape[0] % (bsz * n_seq) == 0 and w_cast2.shape[0] == w_cast.shape[0]
    cast_map = lambda b, s: (b * n_seq + s, 0, 0)
    cast_spec = pl.BlockSpec((w_cast.shape[0] // (bsz * n_seq),) + w_cast.shape[1:], cast_map)
    cast2_spec = pl.BlockSpec((w_cast2.shape[0] // (bsz * n_seq),) + w_cast2.shape[1:], cast_map)
    kern = functools.partial(_mix_kernel, ts=ts, aw=aw, alpha=alpha)
    const = lambda b, s: (0, 0)
    resident = pl.Buffered(1)
    return pl.pallas_call(
        kern,
        grid=(bsz, seq // ts),
        in_specs=[
            pl.BlockSpec(memory_space=pltpu.SMEM),
            pl.BlockSpec((None, ts, d), lambda b, s: (b, s, 0)),
            pl.BlockSpec((None, 6, d), lambda b, s: (b, 0, 0)),
            pl.BlockSpec(w_in.shape, const, pipeline_mode=resident),
            pl.BlockSpec(conv_w.shape, const),
            pl.BlockSpec(conv_p.shape, const),
            pl.BlockSpec(w_o.shape, const, pipeline_mode=resident),
            pl.BlockSpec(ln.shape, const),
            cast_spec,
            cast2_spec,
        ],
        out_specs=[pl.BlockSpec((None, ts, d), lambda b, s: (b, s, 0)), cast_spec, cast2_spec],
        out_shape=[jax.ShapeDtypeStruct((bsz, seq, d), F32), jax.ShapeDtypeStruct(w_cast.shape, BF16),
                   jax.ShapeDtypeStruct(w_cast2.shape, BF16)],
        scratch_shapes=[
            pltpu.VMEM((ts, aw), BF16),
            pltpu.VMEM((4, ts + WINDOW, LANES), BF16),
            pltpu.VMEM((4, ts + WINDOW, LANES), BF16),
            pltpu.VMEM((ts + CONV_HIST, cw), F32),
            pltpu.VMEM((SUBLANES - 1, ts + CONV_HIST - SUBLANES, cw), F32),
            pltpu.VMEM((ts, d), BF16),
        ],
        compiler_params=pltpu.CompilerParams(
            dimension_semantics=("arbitrary", "arbitrary"), vmem_limit_bytes=VMEM_LIMIT),
        name="mix",
    )(sinks, x, mod, w_in, conv_w, conv_p, w_o, ln, w_cast, w_cast2)


def _ffn_pre_kernel(x_ref, mod_ref, wgs_ref, wus_ref, wds_ref, wrt_ref, rb_ref, wca_ref,
                    h_ref, sh_ref, ei_ref, ew_ref, cnt_ref, wca_out_ref):
    wca_out_ref[...] = wca_ref[...].astype(BF16)
    tm = x_ref.shape[0]
    n_exp = wrt_ref.shape[0]
    per = n_exp // N_EXPERT_GROUPS
    hf = x_ref[...] * (1.0 + mod_ref[4:5, :]) + mod_ref[3:4, :]
    _to_slab(h_ref, hf)
    h = hf.astype(BF16)

    gate = jnp.dot(h, wgs_ref[...], preferred_element_type=F32)
    up = jnp.dot(h, wus_ref[...], preferred_element_type=F32)
    act = (gate * jax.nn.sigmoid(gate) * up).astype(BF16)
    sh_ref[...] = jnp.dot(act, wds_ref[...], preferred_element_type=F32)

    logits = lax.dot_general(wrt_ref[...], h, (((1,), (1,)), ((), ())), preferred_element_type=F32)
    scores = jax.nn.sigmoid(logits)
    sel = scores + rb_ref[...]

    iota_p = lax.broadcasted_iota(jnp.int32, (per, tm), 0).astype(F32)
    gs_rows = []
    for g in range(N_EXPERT_GROUPS):
        blk = sel[g * per:(g + 1) * per, :]
        m1 = jnp.max(blk, axis=0, keepdims=True)
        i1 = jnp.min(jnp.where(blk == m1, iota_p, float(per)), axis=0, keepdims=True)
        m2 = jnp.max(jnp.where(iota_p == i1, NEG_INF, blk), axis=0, keepdims=True)
        gs_rows.append(m1 + m2)
    gs = jnp.concatenate(gs_rows, axis=0)
    iota_g = lax.broadcasted_iota(jnp.int32, (N_EXPERT_GROUPS, tm), 0).astype(F32)
    gmask = jnp.zeros((N_EXPERT_GROUPS, tm), jnp.bool_)
    for _ in range(TOPK_EXPERT_GROUPS):
        m = jnp.max(gs, axis=0, keepdims=True)
        gi = jnp.min(jnp.where(gs == m, iota_g, float(N_EXPERT_GROUPS)), axis=0, keepdims=True)
        hit = iota_g == gi
        gmask = gmask | hit
        gs = jnp.where(hit, NEG_INF, gs)
    emask = jnp.concatenate(
        [jnp.broadcast_to(gmask[g:g + 1, :], (per, tm)) for g in range(N_EXPERT_GROUPS)], axis=0)
    cand = jnp.where(emask, sel, NEG_INF)

    iota_e = lax.broadcasted_iota(jnp.int32, (n_exp, tm), 0).astype(F32)
    idx_rows, w_rows = [], []
    for _ in range(TOP_K):
        m = jnp.max(cand, axis=0, keepdims=True)
        ei = jnp.min(jnp.where(cand == m, iota_e, float(n_exp)), axis=0, keepdims=True)
        hit = iota_e == ei
        w_rows.append(jnp.sum(jnp.where(hit, scores, 0.0), axis=0, keepdims=True))
        idx_rows.append(ei)
        cand = jnp.where(hit, NEG_INF, cand)
    wk = jnp.concatenate(w_rows, axis=0)
    ew_ref[...] = wk / jnp.sum(wk, axis=0, keepdims=True) * ROUTED_SCALE
    ei_ref[...] = jnp.concatenate(idx_rows, axis=0).astype(jnp.int32)
    chosen = jnp.where(emask & (cand == NEG_INF), 1.0, 0.0).astype(BF16)
    cnt_ref[...] = jnp.dot(chosen, jnp.ones((tm, LANES), BF16), preferred_element_type=F32)


def _ffn_pre(x1, mod, wgs, wus, wds, wrt, rbias, w_cast, seq):
    n_tok, d = x1.shape
    tm = min(TOK_TILE, seq)
    n_steps = n_tok // tm
    assert seq % tm == 0 and n_tok % tm == 0 and d == SUBLANES * LANES
    assert w_cast.shape[0] % n_steps == 0
    cast_blk = (w_cast.shape[0] // n_steps,) + w_cast.shape[1:]
    cast_spec = pl.BlockSpec(cast_blk, lambda i: (i, 0, 0))
    n_exp = wrt.shape[0]
    const = lambda i: (0, 0)
    return pl.pallas_call(
        _ffn_pre_kernel,
        grid=(n_tok // tm,),
        in_specs=[
            pl.BlockSpec((tm, d), lambda i: (i, 0)),
            pl.BlockSpec((None, 6, d), lambda i: ((i * tm) // seq, 0, 0)),
            pl.BlockSpec(wgs.shape, const),
            pl.BlockSpec(wus.shape, const),
            pl.BlockSpec(wds.shape, const),
            pl.BlockSpec(wrt.shape, const),
            pl.BlockSpec(rbias.shape, const),
            cast_spec,
        ],
        out_specs=[
            pl.BlockSpec((tm * SUBLANES, LANES), lambda i: (i, 0)),
            pl.BlockSpec((tm, d), lambda i: (i, 0)),
            pl.BlockSpec((TOP_K, tm), lambda i: (0, i)),
            pl.BlockSpec((TOP_K, tm), lambda i: (0, i)),
            pl.BlockSpec((None, n_exp, LANES), lambda i: (i, 0, 0)),
            cast_spec,
        ],
        out_shape=[
            jax.ShapeDtypeStruct((n_tok * SUBLANES, LANES), F32),
            jax.ShapeDtypeStruct((n_tok, d), F32),
            jax.ShapeDtypeStruct((TOP_K, n_tok), jnp.int32),
            jax.ShapeDtypeStruct((TOP_K, n_tok), F32),
            jax.ShapeDtypeStruct((n_tok // tm, n_exp, LANES), F32),
            jax.ShapeDtypeStruct(w_cast.shape, BF16),
        ],
        compiler_params=pltpu.CompilerParams(
            dimension_semantics=("arbitrary",), vmem_limit_bytes=VMEM_LIMIT),
        name="ffn_pre",
    )(x1, mod, wgs, wus, wds, wrt, rbias, w_cast)


def _moe_kernel(off_ref, cnt_ref, tok_ref, w_ref, h_hbm, wg_ref, wu_ref, wd_ref, out_hbm,
                h_vmem, acc_ref, xa_ref, xb_ref, xd_ref, ya_ref, yb_ref, yd_ref, sem, *, tt):
    j = pl.program_id(0)
    i = pl.program_id(1)
    n_grp = pl.num_programs(1)
    n_exp = n_grp * MOE_EXPERTS_PER_STEP
    rows = xa_ref.shape[0] // SUBLANES
    seg = MOE_SEG
    last = pl.num_programs(0) * n_exp - 1
    row_iota = lax.broadcasted_iota(jnp.int32, (rows, 1), 0)

    def slab(r):
        if isinstance(r, int):
            return pl.ds(r * SUBLANES, SUBLANES)
        return pl.ds(pl.multiple_of(r * SUBLANES, SUBLANES), SUBLANES)

    def gather_rows(x_ref, base, row_ids):
        for r in row_ids:
            x_ref[slab(r), :] = h_vmem[pl.ds(pl.multiple_of(tok_ref[0, 0, base + r], SUBLANES), SUBLANES), :]

    def scatter_rows(y_ref, base, row_ids):
        for g in range(0, len(row_ids), RMW_UNROLL):
            updates = []
            for r in row_ids[g:g + RMW_UNROLL]:
                dst = pl.ds(pl.multiple_of(tok_ref[0, 0, base + r], SUBLANES), SUBLANES)
                updates.append((dst, acc_ref[dst, :] + w_ref[0, 0, base + r] * y_ref[slab(r), :]))
            for dst, val in reversed(updates):
                acc_ref[dst, :] = val

    def expert_mlp(x_ref, y_ref, k, n_valid):
        x = _from_slab(x_ref, rows).astype(BF16)
        gate = jnp.dot(x, wg_ref[k], preferred_element_type=F32)
        up = jnp.dot(x, wu_ref[k], preferred_element_type=F32)
        act = (gate * jax.nn.sigmoid(gate) * up).astype(BF16)
        y = jnp.dot(act, wd_ref[k], preferred_element_type=F32)
        _to_slab(y_ref, jnp.where(row_iota < jnp.minimum(n_valid, seg), y, 0.0))

    def gather_loop(x_ref, base):
        def body(g, carry):
            gather_rows(x_ref, base, [g * SUBLANES + u for u in range(SUBLANES)])
            return carry
        lax.fori_loop(0, seg // SUBLANES, body, 0)

    def scatter_loop(y_ref, base, n_rows):
        def body(g, carry):
            scatter_rows(y_ref, base, [g * RMW_UNROLL + u for u in range(RMW_UNROLL)])
            return carry
        lax.fori_loop(0, (jnp.minimum(n_rows, seg) + RMW_UNROLL - 1) // RMW_UNROLL, body, 0)

    idx0 = j * n_exp + i * MOE_EXPERTS_PER_STEP

    @pl.when(i == 0)
    def _():
        load = pltpu.make_async_copy(h_hbm.at[pl.ds(j * (tt * SUBLANES), tt * SUBLANES)], h_vmem, sem)
        load.start()
        acc_ref[...] = jnp.zeros_like(acc_ref)
        for ref in (xa_ref, xb_ref, xd_ref, ya_ref, yb_ref):
            ref[...] = jnp.zeros_like(ref)
        load.wait()
        gather_loop(xa_ref, off_ref[idx0])

    static_rows = list(range(seg))
    bufs = ((xa_ref, xb_ref, ya_ref, yb_ref), (xb_ref, xa_ref, yb_ref, ya_ref))
    for k in range(MOE_EXPERTS_PER_STEP):
        x_cur, x_next, y_cur, y_prev = bufs[k % 2]
        idx = idx0 + k
        scatter_rows(y_prev, off_ref[jnp.maximum(idx - 1, 0)], static_rows)
        expert_mlp(x_cur, y_cur, k, cnt_ref[idx])
        gather_rows(x_next, off_ref[jnp.minimum(idx + 1, last)], static_rows)

    for k in range(MOE_EXPERTS_PER_STEP):
        off = off_ref[idx0 + k]
        cnt = cnt_ref[idx0 + k]

        def extra(c, carry):
            gather_loop(xd_ref, off + c * seg)
            expert_mlp(xd_ref, yd_ref, k, cnt - c * seg)
            scatter_loop(yd_ref, off + c * seg, cnt - c * seg)
            return carry

        lax.fori_loop(1, (cnt + seg - 1) // seg, extra, 0)

    @pl.when(i == n_grp - 1)
    def _():
        idx_l = idx0 + MOE_EXPERTS_PER_STEP - 1
        scatter_loop(bufs[(MOE_EXPERTS_PER_STEP - 1) % 2][2], off_ref[idx_l], cnt_ref[idx_l])
        store = pltpu.make_async_copy(acc_ref, out_hbm.at[pl.ds(j * (tt * SUBLANES), tt * SUBLANES)], sem)
        store.start()
        store.wait()


def _moe(h_slab, tok_s, w_s, off, cnt, w_gate, w_up, w_down, tt):
    n_tok = h_slab.shape[0] // SUBLANES
    n_exp, d, f = w_gate.shape
    n_super = n_tok // tt
    rows = MOE_ROWS
    eb = MOE_EXPERTS_PER_STEP
    assert rows % (2 * SUBLANES) == 0 and n_exp % eb == 0 and eb % 2 == 0
    assert MOE_SEG <= rows and MOE_SEG % SUBLANES == 0 and MOE_SEG % RMW_UNROLL == 0
    assert tok_s.shape[2] >= tt * TOP_K + rows
    w_map = lambda j, i, off, cnt: (i, 0, 0)
    lst_map = lambda j, i, off, cnt: (j, 0, 0)
    return pl.pallas_call(
        functools.partial(_moe_kernel, tt=tt),
        grid_spec=pltpu.PrefetchScalarGridSpec(
            num_scalar_prefetch=2,
            grid=(n_super, n_exp // eb),
            in_specs=[
                pl.BlockSpec((1, 1, tok_s.shape[2]), lst_map, memory_space=pltpu.SMEM),
                pl.BlockSpec((1, 1, w_s.shape[2]), lst_map, memory_space=pltpu.SMEM),
                pl.BlockSpec(memory_space=pl.ANY),
                pl.BlockSpec((eb, d, f), w_map),
                pl.BlockSpec((eb, d, f), w_map),
                pl.BlockSpec((eb, f, d), w_map),
            ],
            out_specs=pl.BlockSpec(memory_space=pl.ANY),
            scratch_shapes=[
                pltpu.VMEM((tt * SUBLANES, LANES), F32),
                pltpu.VMEM((tt * SUBLANES, LANES), F32),
                *[pltpu.VMEM((rows * SUBLANES, LANES), F32) for _ in range(6)],
                pltpu.SemaphoreType.DMA,
            ],
        ),
        out_shape=jax.ShapeDtypeStruct((n_tok * SUBLANES, LANES), F32),
        compiler_params=pltpu.CompilerParams(
            dimension_semantics=("arbitrary", "arbitrary"), vmem_limit_bytes=VMEM_LIMIT),
        name="moe",
    )(off, cnt, tok_s, w_s, h_slab, w_gate, w_up, w_down)


def _dispatch_lists(eidx_t, ew_t, counts, tt):
    n_tok = eidx_t.shape[1]
    n_tiles, n_exp = counts.shape[0], counts.shape[1]
    n_super = n_tok // tt
    key = eidx_t * tt + jnp.arange(n_tok, dtype=jnp.int32) % tt

    def by_super_tile(a):
        return a.reshape(TOP_K, n_super, tt).transpose(1, 0, 2).reshape(n_super, TOP_K * tt)

    key_s, w_s = lax.sort((by_super_tile(key), by_super_tile(ew_t)), dimension=1, num_keys=1)
    tok_s = (key_s % tt) * SUBLANES
    pad = ((0, 0), (0, 0), (0, MOE_LIST_PAD))
    tok_s = jnp.pad(tok_s.reshape(n_super, 1, tt * TOP_K), pad)
    w_s = jnp.pad(w_s.reshape(n_super, 1, tt * TOP_K), pad)
    cnt = counts[:, :, 0].astype(jnp.int32).reshape(n_super, n_tiles // n_super, n_exp).sum(axis=1)
    off = jnp.cumsum(cnt, axis=1) - cnt
    return tok_s, w_s, off.reshape(-1), cnt.reshape(-1)


def _final_kernel(x_ref, sh_ref, ffn_ref, mod_ref, ln_ref, o_ref, *, alpha):
    tm = x_ref.shape[0]
    ffn = sh_ref[...] + _from_slab(ffn_ref, tm)
    z = alpha * x_ref[...] + (1.0 + mod_ref[5:6, :]) * ffn
    o_ref[...] = _layer_norm_rows(z, ln_ref[0:1, :], ln_ref[1:2, :])


def _final(x1, shared, routed_slab, mod, ln, seq, alpha):
    n_tok, d = x1.shape
    tm = min(FIN_TILE, seq)
    assert seq % tm == 0
    return pl.pallas_call(
        functools.partial(_final_kernel, alpha=alpha),
        grid=(n_tok // tm,),
        in_specs=[
            pl.BlockSpec((tm, d), lambda i: (i, 0)),
            pl.BlockSpec((tm, d), lambda i: (i, 0)),
            pl.BlockSpec((tm * SUBLANES, LANES), lambda i: (i, 0)),
            pl.BlockSpec((None, 6, d), lambda i: ((i * tm) // seq, 0, 0)),
            pl.BlockSpec(ln.shape, lambda i: (0, 0)),
        ],
        out_specs=pl.BlockSpec((tm, d), lambda i: (i, 0)),
        out_shape=jax.ShapeDtypeStruct((n_tok, d), F32),
        compiler_params=pltpu.CompilerParams(
            dimension_semantics=("arbitrary",), vmem_limit_bytes=VMEM_LIMIT),
        name="final",
    )(x1, shared, routed_slab, mod, ln)


def kernel(x, c, w_ada, b_ada, w_in, sinks, conv_w, conv_b, conv_ln_g, conv_ln_b, w_o, ln1_g, ln1_b,
           w_router, router_bias, w_gate_e, w_up_e, w_down_e, w_gate_s, w_up_s, w_down_s, ln2_g, ln2_b):
    bsz, seq, d = x.shape
    depth = w_ada.shape[0]
    n_exp = w_router.shape[2]
    n_tok = bsz * seq
    alpha = (2.0 * depth) ** 0.25
    tt = min(SUPER_TILE, n_tok)
    assert n_tok % tt == 0 and tt % min(TOK_TILE, seq) == 0

    for l in range(depth):
        mod = _ada(c, w_ada[l], b_ada[l]).reshape(bsz, 6, d)
        conv_p = jnp.stack([conv_b[l], conv_ln_g[l], conv_ln_b[l]])
        x1, wd_bf, wu_bf = _mix(x, mod, w_in[l].astype(BF16), sinks[l], conv_w[l], conv_p,
                                w_o[l].astype(BF16), jnp.stack([ln1_g[l], ln1_b[l]]),
                                w_down_e[l], w_up_e[l], alpha)
        x1 = x1.reshape(n_tok, d)
        h_slab, shared, eidx_t, ew_t, counts, wg_bf = _ffn_pre(
            x1, mod, w_gate_s[l].astype(BF16), w_up_s[l].astype(BF16), w_down_s[l].astype(BF16),
            w_router[l].T.astype(BF16), router_bias[l].reshape(n_exp, 1), w_gate_e[l], seq)
        tok_s, w_s, off, cnt = _dispatch_lists(eidx_t, ew_t, counts, tt)
        routed = _moe(h_slab, tok_s, w_s, off, cnt, wg_bf, wu_bf, wd_bf, tt)
        x = _final(x1, shared, routed, mod, jnp.stack([ln2_g[l], ln2_b[l]]), seq, alpha)
        x = x.reshape(bsz, seq, d)
    return x
```

```python
import functools
import math

import jax
import jax.numpy as jnp
from jax import lax
from jax.experimental import pallas as pl
from jax.experimental.pallas import tpu as pltpu

F32 = jnp.float32
BF16 = jnp.bfloat16
NEG_INF = float("-inf")

HEAD_DIM = 64
WINDOW = 128
CONV_KERNEL = 31
CONV_HIST = 32
TOP_K = 8
N_EXPERT_GROUPS = 8
TOPK_EXPERT_GROUPS = 4
ROUTED_SCALE = 2.5
LN_EPS = 1e-5

LANES = 128
SUBLANES = 8
SEQ_TILE = 512
TOK_TILE = 512
FIN_TILE = 512
SUPER_TILE = 4096
MOE_ROWS = 192
MOE_SEG = 168
MOE_EXPERTS_PER_STEP = 4
MOE_LIST_PAD = 256
RMW_UNROLL = 8
CONV_ROWS = 64
VMEM_LIMIT = 56 * 1024 * 1024


def _alibi_slopes(n_heads):
    return [2.0 ** (-8.0 * (i + 1) / n_heads) for i in range(n_heads)]


def _layer_norm_rows(z, g, b):
    mu = jnp.mean(z, axis=-1, keepdims=True)
    d = z - mu
    var = jnp.mean(d * d, axis=-1, keepdims=True)
    return d * lax.rsqrt(var + LN_EPS) * g + b


def _to_slab(ref, val):
    n = val.shape[0]
    for s in range(SUBLANES):
        ref[pl.ds(s, n, stride=SUBLANES), :] = val[:, s * LANES:(s + 1) * LANES]


def _from_slab(ref, n):
    return jnp.concatenate([ref[pl.ds(s, n, stride=SUBLANES), :] for s in range(SUBLANES)], axis=1)


def _ada_kernel(c_ref, w_ref, b_ref, o_ref):
    c = c_ref[...]
    ca = (c * jax.nn.sigmoid(c)).astype(BF16)
    o_ref[...] = jnp.dot(ca, w_ref[...].astype(BF16), preferred_element_type=F32) + b_ref[...]


def _ada(c, w_ada, b_ada):
    bsz, d = c.shape
    n_out = w_ada.shape[1]
    return pl.pallas_call(
        _ada_kernel,
        grid=(n_out // d,),
        in_specs=[
            pl.BlockSpec((bsz, d), lambda j: (0, 0)),
            pl.BlockSpec((d, d), lambda j: (0, j)),
            pl.BlockSpec((1, d), lambda j: (0, j)),
        ],
        out_specs=pl.BlockSpec((bsz, d), lambda j: (0, j)),
        out_shape=jax.ShapeDtypeStruct((bsz, n_out), F32),
        compiler_params=pltpu.CompilerParams(vmem_limit_bytes=VMEM_LIMIT),
        name="ada",
    )(c, w_ada, b_ada.reshape(1, n_out))


def _mix_kernel(sinks_ref, x_ref, mod_ref, win_ref, convw_ref, convp_ref, wo_ref, ln_ref, wca_ref, wcb_ref,
                o_ref, wca_out_ref, wcb_out_ref, q_ref, ke_ref, ve_ref, glu_ref, gsh_ref, cat_ref,
                *, ts, aw, alpha):
    wca_out_ref[...] = wca_ref[...].astype(BF16)
    wcb_out_ref[...] = wcb_ref[...].astype(BF16)
    s_idx = pl.program_id(1)
    n_heads = aw // HEAD_DIM
    slopes = _alibi_slopes(n_heads)
    cw = cat_ref.shape[1] - aw

    @pl.when(s_idx == 0)
    def _():
        ke_ref[:, 0:WINDOW, :] = jnp.zeros((4, WINDOW, LANES), BF16)
        ve_ref[:, 0:WINDOW, :] = jnp.zeros((4, WINDOW, LANES), BF16)
        glu_ref[0:CONV_HIST, :] = jnp.zeros((CONV_HIST, cw), F32)

    @pl.when(s_idx > 0)
    def _():
        ke_ref[:, 0:WINDOW, :] = ke_ref[:, ts:ts + WINDOW, :]
        ve_ref[:, 0:WINDOW, :] = ve_ref[:, ts:ts + WINDOW, :]
        glu_ref[0:CONV_HIST, :] = glu_ref[ts:ts + CONV_HIST, :]

    x = x_ref[...]
    h = (x * (1.0 + mod_ref[1:2, :]) + mod_ref[0:1, :]).astype(BF16)

    q = jnp.dot(h, win_ref[:, 0:aw], preferred_element_type=F32)
    q_ref[...] = (q * (1.0 / math.sqrt(HEAD_DIM))).astype(BF16)
    kv = jnp.dot(h, win_ref[:, aw:aw + 2 * LANES], preferred_element_type=F32)
    lo = lax.broadcasted_iota(jnp.int32, (ts, LANES), 1) < HEAD_DIM
    for dst, t in ((ke_ref, kv[:, 0:LANES]), (ve_ref, kv[:, LANES:2 * LANES])):
        t_r = pltpu.roll(t, HEAD_DIM, axis=1)
        dst[0, WINDOW:WINDOW + ts, :] = jnp.where(lo, t, 0.0).astype(BF16)
        dst[1, WINDOW:WINDOW + ts, :] = jnp.where(lo, 0.0, t_r).astype(BF16)
        dst[2, WINDOW:WINDOW + ts, :] = jnp.where(lo, t_r, 0.0).astype(BF16)
        dst[3, WINDOW:WINDOW + ts, :] = jnp.where(lo, 0.0, t).astype(BF16)
    u0 = aw + 2 * LANES
    ga = jnp.dot(h, win_ref[:, u0:u0 + cw], preferred_element_type=F32)
    gb = jnp.dot(h, win_ref[:, u0 + cw:u0 + 2 * cw], preferred_element_type=F32)
    glu_ref[CONV_HIST:CONV_HIST + ts, :] = ga * jax.nn.sigmoid(gb)

    qi = lax.broadcasted_iota(jnp.int32, (WINDOW, 2 * WINDOW), 0)
    kj = lax.broadcasted_iota(jnp.int32, (WINDOW, 2 * WINDOW), 1)
    dist = WINDOW + qi - kj
    band = (dist >= 0) & (dist < WINDOW)
    neg_dist = -dist.astype(F32)
    bias_any = jnp.where(band, neg_dist, NEG_INF)
    bias_first = jnp.where(band & ((kj >= WINDOW) | (s_idx > 0)), neg_dist, NEG_INF)
    for i in range(ts // WINDOW):
        r0 = i * WINDOW
        bias = bias_first if i == 0 else bias_any
        for pair in range(aw // LANES):
            g = (2 * pair) // (n_heads // 2)
            qp = q_ref[r0:r0 + WINDOW, pair * LANES:(pair + 1) * LANES]
            out_pair = None
            for par in range(2):
                hd = 2 * pair + par
                kk = ke_ref[2 * g + par, r0:r0 + 2 * WINDOW, :]
                s = lax.dot_general(qp, kk, (((1,), (1,)), ((), ())), preferred_element_type=F32)
                s = s + slopes[hd] * bias
                sink = sinks_ref[hd]
                m = jnp.maximum(jnp.max(s, axis=-1, keepdims=True), sink)
                p = jnp.exp(s - m)
                denom = jnp.sum(p, axis=-1, keepdims=True) + jnp.exp(sink - m)
                vv = ve_ref[2 * g + par, r0:r0 + 2 * WINDOW, :]
                o = jnp.dot(p.astype(BF16), vv, preferred_element_type=F32) * (1.0 / denom)
                out_pair = o if out_pair is None else out_pair + o
            cat_ref[r0:r0 + WINDOW, pair * LANES:(pair + 1) * LANES] = out_pair.astype(BF16)

    conv_b = convp_ref[0:1, :]
    cln_g = convp_ref[1:2, :]
    cln_b = convp_ref[2:3, :]
    off = CONV_HIST - (CONV_KERNEL - 1)
    n_sh = gsh_ref.shape[1]
    for p in range(1, SUBLANES):
        gsh_ref[p - 1] = glu_ref[p:p + n_sh, :]
    for c in range(ts // CONV_ROWS):
        c0 = c * CONV_ROWS
        acc = jnp.broadcast_to(conv_b, (CONV_ROWS, cw))
        for j in range(CONV_KERNEL):
            a, p = divmod(off + j, SUBLANES)
            r0 = c0 + a * SUBLANES
            tap = glu_ref[r0:r0 + CONV_ROWS, :] if p == 0 else gsh_ref[p - 1, r0:r0 + CONV_ROWS, :]
            acc = acc + tap * convw_ref[j:j + 1, :]
        yn = _layer_norm_rows(acc, cln_g, cln_b)
        cat_ref[c0:c0 + CONV_ROWS, aw:aw + cw] = (yn * jax.nn.sigmoid(yn)).astype(BF16)

    mix = jnp.dot(cat_ref[...], wo_ref[...], preferred_element_type=F32)
    z = alpha * x + (1.0 + mod_ref[2:3, :]) * mix
    o_ref[...] = _layer_norm_rows(z, ln_ref[0:1, :], ln_ref[1:2, :])


def _mix(x, mod, w_in, sinks, conv_w, conv_p, w_o, ln, w_cast, w_cast2, alpha):
    bsz, seq, d = x.shape
    cw = conv_w.shape[1]
    aw = d - cw
    ts = min(SEQ_TILE, seq)
    n_seq = seq // ts
    assert seq % ts == 0 and ts % WINDOW == 0 and aw % LANES == 0
    assert (aw // HEAD_DIM) // 4 == 2, "kernel packs exactly two KV heads into one lane group"
    assert w_in.shape[1] == aw + 2 * LANES + 2 * cw
    assert w_cast.shape[0] % (bsz * n_seq) == 0 and w_cast2.shape[0] == w_cast.shape[0]
    cast_map = lambda b, s: (b * n_seq + s, 0, 0)
    cast_spec = pl.BlockSpec((w_cast.shape[0] // (bsz * n_seq),) + w_cast.shape[1:], cast_map)
    cast2_spec = pl.BlockSpec((w_cast2.shape[0] // (bsz * n_seq),) + w_cast2.shape[1:], cast_map)
    kern = functools.partial(_mix_kernel, ts=ts, aw=aw, alpha=alpha)
    const = lambda b, s: (0, 0)
    resident = pl.Buffered(1)
    return pl.pallas_call(
        kern,
        grid=(bsz, seq // ts),
        in_specs=[
            pl.BlockSpec(memory_space=pltpu.SMEM),
            pl.BlockSpec((None, ts, d), lambda b, s: (b, s, 0)),
            pl.BlockSpec((None, 6, d), lambda b, s: (b, 0, 0)),
            pl.BlockSpec(w_in.shape, const, pipeline_mode=resident),
            pl.BlockSpec(conv_w.shape, const),
            pl.BlockSpec(conv_p.shape, const),
            pl.BlockSpec(w_o.shape, const, pipeline_mode=resident),
            pl.BlockSpec(ln.shape, const),
            cast_spec,
            cast2_spec,
        ],
        out_specs=[pl.BlockSpec((None, ts, d), lambda b, s: (b, s, 0)), cast_spec, cast2_spec],
        out_shape=[jax.ShapeDtypeStruct((bsz, seq, d), F32), jax.ShapeDtypeStruct(w_cast.shape, BF16),
                   jax.ShapeDtypeStruct(w_cast2.shape, BF16)],
        scratch_shapes=[
            pltpu.VMEM((ts, aw), BF16),
            pltpu.VMEM((4, ts + WINDOW, LANES), BF16),
            pltpu.VMEM((4, ts + WINDOW, LANES), BF16),
            pltpu.VMEM((ts + CONV_HIST, cw), F32),
            pltpu.VMEM((SUBLANES - 1, ts + CONV_HIST - SUBLANES, cw), F32),
            pltpu.VMEM((ts, d), BF16),
        ],
        compiler_params=pltpu.CompilerParams(
            dimension_semantics=("arbitrary", "arbitrary"), vmem_limit_bytes=VMEM_LIMIT),
        name="mix",
    )(sinks, x, mod, w_in, conv_w, conv_p, w_o, ln, w_cast, w_cast2)


def _ffn_pre_kernel(x_ref, mod_ref, wgs_ref, wus_ref, wds_ref, wrt_ref, rb_ref, wca_ref,
                    h_ref, sh_ref, ei_ref, ew_ref, cnt_ref, wca_out_ref):
    wca_out_ref[...] = wca_ref[...].astype(BF16)
    tm = x_ref.shape[0]
    n_exp = wrt_ref.shape[0]
    per = n_exp // N_EXPERT_GROUPS
    hf = x_ref[...] * (1.0 + mod_ref[4:5, :]) + mod_ref[3:4, :]
    _to_slab(h_ref, hf)
    h = hf.astype(BF16)

    gate = jnp.dot(h, wgs_ref[...], preferred_element_type=F32)
    up = jnp.dot(h, wus_ref[...], preferred_element_type=F32)
    act = (gate * jax.nn.sigmoid(gate) * up).astype(BF16)
    sh_ref[...] = jnp.dot(act, wds_ref[...], preferred_element_type=F32)

    logits = lax.dot_general(wrt_ref[...], h, (((1,), (1,)), ((), ())), preferred_element_type=F32)
    scores = jax.nn.sigmoid(logits)
    sel = scores + rb_ref[...]

    iota_p = lax.broadcasted_iota(jnp.int32, (per, tm), 0).astype(F32)
    gs_rows = []
    for g in range(N_EXPERT_GROUPS):
        blk = sel[g * per:(g + 1) * per, :]
        m1 = jnp.max(blk, axis=0, keepdims=True)
        i1 = jnp.min(jnp.where(blk == m1, iota_p, float(per)), axis=0, keepdims=True)
        m2 = jnp.max(jnp.where(iota_p == i1, NEG_INF, blk), axis=0, keepdims=True)
        gs_rows.append(m1 + m2)
    gs = jnp.concatenate(gs_rows, axis=0)
    iota_g = lax.broadcasted_iota(jnp.int32, (N_EXPERT_GROUPS, tm), 0).astype(F32)
    gmask = jnp.zeros((N_EXPERT_GROUPS, tm), jnp.bool_)
    for _ in range(TOPK_EXPERT_GROUPS):
        m = jnp.max(gs, axis=0, keepdims=True)
        gi = jnp.min(jnp.where(gs == m, iota_g, float(N_EXPERT_GROUPS)), axis=0, keepdims=True)
        hit = iota_g == gi
        gmask = gmask | hit
        gs = jnp.where(hit, NEG_INF, gs)
    emask = jnp.concatenate(
        [jnp.broadcast_to(gmask[g:g + 1, :], (per, tm)) for g in range(N_EXPERT_GROUPS)], axis=0)
    cand = jnp.where(emask, sel, NEG_INF)

    iota_e = lax.broadcasted_iota(jnp.int32, (n_exp, tm), 0).astype(F32)
    idx_rows, w_rows = [], []
    for _ in range(TOP_K):
        m = jnp.max(cand, axis=0, keepdims=True)
        ei = jnp.min(jnp.where(cand == m, iota_e, float(n_exp)), axis=0, keepdims=True)
        hit = iota_e == ei
        w_rows.append(jnp.sum(jnp.where(hit, scores, 0.0), axis=0, keepdims=True))
        idx_rows.append(ei)
        cand = jnp.where(hit, NEG_INF, cand)
    wk = jnp.concatenate(w_rows, axis=0)
    ew_ref[...] = wk / jnp.sum(wk, axis=0, keepdims=True) * ROUTED_SCALE
    ei_ref[...] = jnp.concatenate(idx_rows, axis=0).astype(jnp.int32)
    chosen = jnp.where(emask & (cand == NEG_INF), 1.0, 0.0).astype(BF16)
    cnt_ref[...] = jnp.dot(chosen, jnp.ones((tm, LANES), BF16), preferred_element_type=F32)


def _ffn_pre(x1, mod, wgs, wus, wds, wrt, rbias, w_cast, seq):
    n_tok, d = x1.shape
    tm = min(TOK_TILE, seq)
    n_steps = n_tok // tm
    assert seq % tm == 0 and n_tok % tm == 0 and d == SUBLANES * LANES
    assert w_cast.shape[0] % n_steps == 0
    cast_blk = (w_cast.shape[0] // n_steps,) + w_cast.shape[1:]
    cast_spec = pl.BlockSpec(cast_blk, lambda i: (i, 0, 0))
    n_exp = wrt.shape[0]
    const = lambda i: (0, 0)
    return pl.pallas_call(
        _ffn_pre_kernel,
        grid=(n_tok // tm,),
        in_specs=[
            pl.BlockSpec((tm, d), lambda i: (i, 0)),
            pl.BlockSpec((None, 6, d), lambda i: ((i * tm) // seq, 0, 0)),
            pl.BlockSpec(wgs.shape, const),
            pl.BlockSpec(wus.shape, const),
            pl.BlockSpec(wds.shape, const),
            pl.BlockSpec(wrt.shape, const),
            pl.BlockSpec(rbias.shape, const),
            cast_spec,
        ],
        out_specs=[
            pl.BlockSpec((tm * SUBLANES, LANES), lambda i: (i, 0)),
            pl.BlockSpec((tm, d), lambda i: (i, 0)),
            pl.BlockSpec((TOP_K, tm), lambda i: (0, i)),
            pl.BlockSpec((TOP_K, tm), lambda i: (0, i)),
            pl.BlockSpec((None, n_exp, LANES), lambda i: (i, 0, 0)),
            cast_spec,
        ],
        out_shape=[
            jax.ShapeDtypeStruct((n_tok * SUBLANES, LANES), F32),
            jax.ShapeDtypeStruct((n_tok, d), F32),
            jax.ShapeDtypeStruct((TOP_K, n_tok), jnp.int32),
            jax.ShapeDtypeStruct((TOP_K, n_tok), F32),
            jax.ShapeDtypeStruct((n_tok // tm, n_exp, LANES), F32),
            jax.ShapeDtypeStruct(w_cast.shape, BF16),
        ],
        compiler_params=pltpu.CompilerParams(
            dimension_semantics=("arbitrary",), vmem_limit_bytes=VMEM_LIMIT),
        name="ffn_pre",
    )(x1, mod, wgs, wus, wds, wrt, rbias, w_cast)


def _moe_kernel(off_ref, cnt_ref, tok_ref, w_ref, h_hbm, wg_ref, wu_ref, wd_ref, out_hbm,
                h_vmem, acc_ref, xa_ref, xb_ref, xd_ref, ya_ref, yb_ref, yd_ref, sem, *, tt):
    j = pl.program_id(0)
    i = pl.program_id(1)
    n_grp = pl.num_programs(1)
    n_exp = n_grp * MOE_EXPERTS_PER_STEP
    rows = xa_ref.shape[0] // SUBLANES
    seg = MOE_SEG
    last = pl.num_programs(0) * n_exp - 1
    row_iota = lax.broadcasted_iota(jnp.int32, (rows, 1), 0)

    def slab(r):
        if isinstance(r, int):
            return pl.ds(r * SUBLANES, SUBLANES)
        return pl.ds(pl.multiple_of(r * SUBLANES, SUBLANES), SUBLANES)

    def gather_rows(x_ref, base, row_ids):
        for r in row_ids:
            x_ref[slab(r), :] = h_vmem[pl.ds(pl.multiple_of(tok_ref[0, 0, base + r], SUBLANES), SUBLANES), :]

    def scatter_rows(y_ref, base, row_ids):
        for g in range(0, len(row_ids), RMW_UNROLL):
            updates = []
            for r in row_ids[g:g + RMW_UNROLL]:
                dst = pl.ds(pl.multiple_of(tok_ref[0, 0, base + r], SUBLANES), SUBLANES)
                updates.append((dst, acc_ref[dst, :] + w_ref[0, 0, base + r] * y_ref[slab(r), :]))
            for dst, val in reversed(updates):
                acc_ref[dst, :] = val

    def expert_mlp(x_ref, y_ref, k, n_valid):
        x = _from_slab(x_ref, rows).astype(BF16)
        gate = jnp.dot(x, wg_ref[k], preferred_element_type=F32)
        up = jnp.dot(x, wu_ref[k], preferred_element_type=F32)
        act = (gate * jax.nn.sigmoid(gate) * up).astype(BF16)
        y = jnp.dot(act, wd_ref[k], preferred_element_type=F32)
        _to_slab(y_ref, jnp.where(row_iota < jnp.minimum(n_valid, seg), y, 0.0))

    def gather_loop(x_ref, base):
        def body(g, carry):
            gather_rows(x_ref, base, [g * SUBLANES + u for u in range(SUBLANES)])
            return carry
        lax.fori_loop(0, seg // SUBLANES, body, 0)

    def scatter_loop(y_ref, base, n_rows):
        def body(g, carry):
            scatter_rows(y_ref, base, [g * RMW_UNROLL + u for u in range(RMW_UNROLL)])
            return carry
        lax.fori_loop(0, (jnp.minimum(n_rows, seg) + RMW_UNROLL - 1) // RMW_UNROLL, body, 0)

    idx0 = j * n_exp + i * MOE_EXPERTS_PER_STEP

    @pl.when(i == 0)
    def _():
        load = pltpu.make_async_copy(h_hbm.at[pl.ds(j * (tt * SUBLANES), tt * SUBLANES)], h_vmem, sem)
        load.start()
        acc_ref[...] = jnp.zeros_like(acc_ref)
        for ref in (xa_ref, xb_ref, xd_ref, ya_ref, yb_ref):
            ref[...] = jnp.zeros_like(ref)
        load.wait()
        gather_loop(xa_ref, off_ref[idx0])

    static_rows = list(range(seg))
    bufs = ((xa_ref, xb_ref, ya_ref, yb_ref), (xb_ref, xa_ref, yb_ref, ya_ref))
    for k in range(MOE_EXPERTS_PER_STEP):
        x_cur, x_next, y_cur, y_prev = bufs[k % 2]
        idx = idx0 + k
        scatter_rows(y_prev, off_ref[jnp.maximum(idx - 1, 0)], static_rows)
        expert_mlp(x_cur, y_cur, k, cnt_ref[idx])
        gather_rows(x_next, off_ref[jnp.minimum(idx + 1, last)], static_rows)

    for k in range(MOE_EXPERTS_PER_STEP):
        off = off_ref[idx0 + k]
        cnt = cnt_ref[idx0 + k]

        def extra(c, carry):
            gather_loop(xd_ref, off + c * seg)
            expert_mlp(xd_ref, yd_ref, k, cnt - c * seg)
            scatter_loop(yd_ref, off + c * seg, cnt - c * seg)
            return carry

        lax.fori_loop(1, (cnt + seg - 1) // seg, extra, 0)

    @pl.when(i == n_grp - 1)
    def _():
        idx_l = idx0 + MOE_EXPERTS_PER_STEP - 1
        scatter_loop(bufs[(MOE_EXPERTS_PER_STEP - 1) % 2][2], off_ref[idx_l], cnt_ref[idx_l])
        store = pltpu.make_async_copy(acc_ref, out_hbm.at[pl.ds(j * (tt * SUBLANES), tt * SUBLANES)], sem)
        store.start()
        store.wait()


def _moe(h_slab, tok_s, w_s, off, cnt, w_gate, w_up, w_down, tt):
    n_tok = h_slab.shape[0] // SUBLANES
    n_exp, d, f = w_gate.shape
    n_super = n_tok // tt
    rows = MOE_ROWS
    eb = MOE_EXPERTS_PER_STEP
    assert rows % (2 * SUBLANES) == 0 and n_exp % eb == 0 and eb % 2 == 0
    assert MOE_SEG <= rows and MOE_SEG % SUBLANES == 0 and MOE_SEG % RMW_UNROLL == 0
    assert tok_s.shape[2] >= tt * TOP_K + rows
    w_map = lambda j, i, off, cnt: (i, 0, 0)
    lst_map = lambda j, i, off, cnt: (j, 0, 0)
    return pl.pallas_call(
        functools.partial(_moe_kernel, tt=tt),
        grid_spec=pltpu.PrefetchScalarGridSpec(
            num_scalar_prefetch=2,
            grid=(n_super, n_exp // eb),
            in_specs=[
                pl.BlockSpec((1, 1, tok_s.shape[2]), lst_map, memory_space=pltpu.SMEM),
                pl.BlockSpec((1, 1, w_s.shape[2]), lst_map, memory_space=pltpu.SMEM),
                pl.BlockSpec(memory_space=pl.ANY),
                pl.BlockSpec((eb, d, f), w_map),
                pl.BlockSpec((eb, d, f), w_map),
                pl.BlockSpec((eb, f, d), w_map),
            ],
            out_specs=pl.BlockSpec(memory_space=pl.ANY),
            scratch_shapes=[
                pltpu.VMEM((tt * SUBLANES, LANES), F32),
                pltpu.VMEM((tt * SUBLANES, LANES), F32),
                *[pltpu.VMEM((rows * SUBLANES, LANES), F32) for _ in range(6)],
                pltpu.SemaphoreType.DMA,
            ],
        ),
        out_shape=jax.ShapeDtypeStruct((n_tok * SUBLANES, LANES), F32),
        compiler_params=pltpu.CompilerParams(
            dimension_semantics=("arbitrary", "arbitrary"), vmem_limit_bytes=VMEM_LIMIT),
        name="moe",
    )(off, cnt, tok_s, w_s, h_slab, w_gate, w_up, w_down)


def _dispatch_lists(eidx_t, ew_t, counts, tt):
    n_tok = eidx_t.shape[1]
    n_tiles, n_exp = counts.shape[0], counts.shape[1]
    n_super = n_tok // tt
    key = eidx_t * tt + jnp.arange(n_tok, dtype=jnp.int32) % tt

    def by_super_tile(a):
        return a.reshape(TOP_K, n_super, tt).transpose(1, 0, 2).reshape(n_super, TOP_K * tt)

    key_s, w_s = lax.sort((by_super_tile(key), by_super_tile(ew_t)), dimension=1, num_keys=1)
    tok_s = (key_s % tt) * SUBLANES
    pad = ((0, 0), (0, 0), (0, MOE_LIST_PAD))
    tok_s = jnp.pad(tok_s.reshape(n_super, 1, tt * TOP_K), pad)
    w_s = jnp.pad(w_s.reshape(n_super, 1, tt * TOP_K), pad)
    cnt = counts[:, :, 0].astype(jnp.int32).reshape(n_super, n_tiles // n_super, n_exp).sum(axis=1)
    off = jnp.cumsum(cnt, axis=1) - cnt
    return tok_s, w_s, off.reshape(-1), cnt.reshape(-1)


def _final_kernel(x_ref, sh_ref, ffn_ref, mod_ref, ln_ref, o_ref, *, alpha):
    tm = x_ref.shape[0]
    ffn = sh_ref[...] + _from_slab(ffn_ref, tm)
    z = alpha * x_ref[...] + (1.0 + mod_ref[5:6, :]) * ffn
    o_ref[...] = _layer_norm_rows(z, ln_ref[0:1, :], ln_ref[1:2, :])


def _final(x1, shared, routed_slab, mod, ln, seq, alpha):
    n_tok, d = x1.shape
    tm = min(FIN_TILE, seq)
    assert seq % tm == 0
    return pl.pallas_call(
        functools.partial(_final_kernel, alpha=alpha),
        grid=(n_tok // tm,),
        in_specs=[
            pl.BlockSpec((tm, d), lambda i: (i, 0)),
            pl.BlockSpec((tm, d), lambda i: (i, 0)),
            pl.BlockSpec((tm * SUBLANES, LANES), lambda i: (i, 0)),
            pl.BlockSpec((None, 6, d), lambda i: ((i * tm) // seq, 0, 0)),
            pl.BlockSpec(ln.shape, lambda i: (0, 0)),
        ],
        out_specs=pl.BlockSpec((tm, d), lambda i: (i, 0)),
        out_shape=jax.ShapeDtypeStruct((n_tok, d), F32),
        compiler_params=pltpu.CompilerParams(
            dimension_semantics=("arbitrary",), vmem_limit_bytes=VMEM_LIMIT),
        name="final",
    )(x1, shared, routed_slab, mod, ln)


def kernel(x, c, w_ada, b_ada, w_in, sinks, conv_w, conv_b, conv_ln_g, conv_ln_b, w_o, ln1_g, ln1_b,
           w_router, router_bias, w_gate_e, w_up_e, w_down_e, w_gate_s, w_up_s, w_down_s, ln2_g, ln2_b):
    bsz, seq, d = x.shape
    depth = w_ada.shape[0]
    n_exp = w_router.shape[2]
    n_tok = bsz * seq
    alpha = (2.0 * depth) ** 0.25
    tt = min(SUPER_TILE, n_tok)
    assert n_tok % tt == 0 and tt % min(TOK_TILE, seq) == 0

    for l in range(depth):
        mod = _ada(c, w_ada[l], b_ada[l]).reshape(bsz, 6, d)
        conv_p = jnp.stack([conv_b[l], conv_ln_g[l], conv_ln_b[l]])
        x1, wd_bf, wu_bf = _mix(x, mod, w_in[l].astype(BF16), sinks[l], conv_w[l], conv_p,
                                w_o[l].astype(BF16), jnp.stack([ln1_g[l], ln1_b[l]]),
                                w_down_e[l], w_up_e[l], alpha)
        x1 = x1.reshape(n_tok, d)
        h_slab, shared, eidx_t, ew_t, counts, wg_bf = _ffn_pre(
            x1, mod, w_gate_s[l].astype(BF16), w_up_s[l].astype(BF16), w_down_s[l].astype(BF16),
            w_router[l].T.astype(BF16), router_bias[l].reshape(n_exp, 1), w_gate_e[l], seq)
        tok_s, w_s, off, cnt = _dispatch_lists(eidx_t, ew_t, counts, tt)
        routed = _moe(h_slab, tok_s, w_s, off, cnt, wg_bf, wu_bf, wd_bf, tt)
        x = _final(x1, shared, routed, mod, jnp.stack([ln2_g[l], ln2_b[l]]), seq, alpha)
        x = x.reshape(bsz, seq, d)
    return x
```

```python
import functools
import math

import jax
import jax.numpy as jnp
from jax import lax
from jax.experimental import pallas as pl
from jax.experimental.pallas import tpu as pltpu

F32 = jnp.float32
BF16 = jnp.bfloat16
NEG_INF = float("-inf")

HEAD_DIM = 64
WINDOW = 128
CONV_KERNEL = 31
CONV_HIST = 32
TOP_K = 8
N_EXPERT_GROUPS = 8
TOPK_EXPERT_GROUPS = 4
ROUTED_SCALE = 2.5
LN_EPS = 1e-5

LANES = 128
SUBLANES = 8
SEQ_TILE = 512
TOK_TILE = 512
FIN_TILE = 512
SUPER_TILE = 4096
MOE_ROWS = 192
MOE_SEG = 168
MOE_EXPERTS_PER_STEP = 4
MOE_LIST_PAD = 256
RMW_UNROLL = 8
CONV_ROWS = 64
VMEM_LIMIT = 56 * 1024 * 1024


def _alibi_slopes(n_heads):
    return [2.0 ** (-8.0 * (i + 1) / n_heads) for i in range(n_heads)]


def _layer_norm_rows(z, g, b):
    mu = jnp.mean(z, axis=-1, keepdims=True)
    d = z - mu
    var = jnp.mean(d * d, axis=-1, keepdims=True)
    return d * lax.rsqrt(var + LN_EPS) * g + b


def _to_slab(ref, val):
    n = val.shape[0]
    for s in range(SUBLANES):
        ref[pl.ds(s, n, stride=SUBLANES), :] = val[:, s * LANES:(s + 1) * LANES]


def _from_slab(ref, n):
    return jnp.concatenate([ref[pl.ds(s, n, stride=SUBLANES), :] for s in range(SUBLANES)], axis=1)


def _ada_kernel(c_ref, w_ref, b_ref, o_ref):
    c = c_ref[...]
    ca = (c * jax.nn.sigmoid(c)).astype(BF16)
    o_ref[...] = jnp.dot(ca, w_ref[...].astype(BF16), preferred_element_type=F32) + b_ref[...]


def _ada(c, w_ada, b_ada):
    bsz, d = c.shape
    n_out = w_ada.shape[1]
    return pl.pallas_call(
        _ada_kernel,
        grid=(n_out // d,),
        in_specs=[
            pl.BlockSpec((bsz, d), lambda j: (0, 0)),
            pl.BlockSpec((d, d), lambda j: (0, j)),
            pl.BlockSpec((1, d), lambda j: (0, j)),
        ],
        out_specs=pl.BlockSpec((bsz, d), lambda j: (0, j)),
        out_shape=jax.ShapeDtypeStruct((bsz, n_out), F32),
        compiler_params=pltpu.CompilerParams(vmem_limit_bytes=VMEM_LIMIT),
        name="ada",
    )(c, w_ada, b_ada.reshape(1, n_out))


def _mix_kernel(sinks_ref, x_ref, mod_ref, win_ref, convw_ref, convp_ref, wo_ref, ln_ref, wca_ref, wcb_ref,
                o_ref, wca_out_ref, wcb_out_ref, q_ref, ke_ref, ve_ref, glu_ref, gsh_ref, cat_ref,
                *, ts, aw, alpha):
    wca_out_ref[...] = wca_ref[...].astype(BF16)
    wcb_out_ref[...] = wcb_ref[...].astype(BF16)
    s_idx = pl.program_id(1)
    n_heads = aw // HEAD_DIM
    slopes = _alibi_slopes(n_heads)
    cw = cat_ref.shape[1] - aw

    @pl.when(s_idx == 0)
    def _():
        ke_ref[:, 0:WINDOW, :] = jnp.zeros((4, WINDOW, LANES), BF16)
        ve_ref[:, 0:WINDOW, :] = jnp.zeros((4, WINDOW, LANES), BF16)
        glu_ref[0:CONV_HIST, :] = jnp.zeros((CONV_HIST, cw), F32)

    @pl.when(s_idx > 0)
    def _():
        ke_ref[:, 0:WINDOW, :] = ke_ref[:, ts:ts + WINDOW, :]
        ve_ref[:, 0:WINDOW, :] = ve_ref[:, ts:ts + WINDOW, :]
        glu_ref[0:CONV_HIST, :] = glu_ref[ts:ts + CONV_HIST, :]

    x = x_ref[...]
    h = (x * (1.0 + mod_ref[1:2, :]) + mod_ref[0:1, :]).astype(BF16)

    q = jnp.dot(h, win_ref[:, 0:aw], preferred_element_type=F32)
    q_ref[...] = (q * (1.0 / math.sqrt(HEAD_DIM))).astype(BF16)
    kv = jnp.dot(h, win_ref[:, aw:aw + 2 * LANES], preferred_element_type=F32)
    lo = lax.broadcasted_iota(jnp.int32, (ts, LANES), 1) < HEAD_DIM
    for dst, t in ((ke_ref, kv[:, 0:LANES]), (ve_ref, kv[:, LANES:2 * LANES])):
        t_r = pltpu.roll(t, HEAD_DIM, axis=1)
        dst[0, WINDOW:WINDOW + ts, :] = jnp.where(lo, t, 0.0).astype(BF16)
        dst[1, WINDOW:WINDOW + ts, :] = jnp.where(lo, 0.0, t_r).astype(BF16)
        dst[2, WINDOW:WINDOW + ts, :] = jnp.where(lo, t_r, 0.0).astype(BF16)
        dst[3, WINDOW:WINDOW + ts, :] = jnp.where(lo, 0.0, t).astype(BF16)
    u0 = aw + 2 * LANES
    ga = jnp.dot(h, win_ref[:, u0:u0 + cw], preferred_element_type=F32)
    gb = jnp.dot(h, win_ref[:, u0 + cw:u0 + 2 * cw], preferred_element_type=F32)
    glu_ref[CONV_HIST:CONV_HIST + ts, :] = ga * jax.nn.sigmoid(gb)

    qi = lax.broadcasted_iota(jnp.int32, (WINDOW, 2 * WINDOW), 0)
    kj = lax.broadcasted_iota(jnp.int32, (WINDOW, 2 * WINDOW), 1)
    dist = WINDOW + qi - kj
    band = (dist >= 0) & (dist < WINDOW)
    neg_dist = -dist.astype(F32)
    bias_any = jnp.where(band, neg_dist, NEG_INF)
    bias_first = jnp.where(band & ((kj >= WINDOW) | (s_idx > 0)), neg_dist, NEG_INF)
    for i in range(ts // WINDOW):
        r0 = i * WINDOW
        bias = bias_first if i == 0 else bias_any
        for pair in range(aw // LANES):
            g = (2 * pair) // (n_heads // 2)
            qp = q_ref[r0:r0 + WINDOW, pair * LANES:(pair + 1) * LANES]
            out_pair = None
            for par in range(2):
                hd = 2 * pair + par
                kk = ke_ref[2 * g + par, r0:r0 + 2 * WINDOW, :]
                s = lax.dot_general(qp, kk, (((1,), (1,)), ((), ())), preferred_element_type=F32)
                s = s + slopes[hd] * bias
                sink = sinks_ref[hd]
                m = jnp.maximum(jnp.max(s, axis=-1, keepdims=True), sink)
                p = jnp.exp(s - m)
                denom = jnp.sum(p, axis=-1, keepdims=True) + jnp.exp(sink - m)
                vv = ve_ref[2 * g + par, r0:r0 + 2 * WINDOW, :]
                o = jnp.dot(p.astype(BF16), vv, preferred_element_type=F32) * (1.0 / denom)
                out_pair = o if out_pair is None else out_pair + o
            cat_ref[r0:r0 + WINDOW, pair * LANES:(pair + 1) * LANES] = out_pair.astype(BF16)

    conv_b = convp_ref[0:1, :]
    cln_g = convp_ref[1:2, :]
    cln_b = convp_ref[2:3, :]
    off = CONV_HIST - (CONV_KERNEL - 1)
    n_sh = gsh_ref.shape[1]
    for p in range(1, SUBLANES):
        gsh_ref[p - 1] = glu_ref[p:p + n_sh, :]
    for c in range(ts // CONV_ROWS):
        c0 = c * CONV_ROWS
        acc = jnp.broadcast_to(conv_b, (CONV_ROWS, cw))
        for j in range(CONV_KERNEL):
            a, p = divmod(off + j, SUBLANES)
            r0 = c0 + a * SUBLANES
            tap = glu_ref[r0:r0 + CONV_ROWS, :] if p == 0 else gsh_ref[p - 1, r0:r0 + CONV_ROWS, :]
            acc = acc + tap * convw_ref[j:j + 1, :]
        yn = _layer_norm_rows(acc, cln_g, cln_b)
        cat_ref[c0:c0 + CONV_ROWS, aw:aw + cw] = (yn * jax.nn.sigmoid(yn)).astype(BF16)

    mix = jnp.dot(cat_ref[...], wo_ref[...], preferred_element_type=F32)
    z = alpha * x + (1.0 + mod_ref[2:3, :]) * mix
    o_ref[...] = _layer_norm_rows(z, ln_ref[0:1, :], ln_ref[1:2, :])


def _mix(x, mod, w_in, sinks, conv_w, conv_p, w_o, ln, w_cast, w_cast2, alpha):
    bsz, seq, d = x.shape
    cw = conv_w.shape[1]
    aw = d - cw
    ts = min(SEQ_TILE, seq)
    n_seq = seq // ts
    assert seq % ts == 0 and ts % WINDOW == 0 and aw % LANES == 0
    assert (aw // HEAD_DIM) // 4 == 2, "kernel packs exactly two KV heads into one lane group"
    assert w_in.shape[1] == aw + 2 * LANES + 2 * cw
    assert w_cast.shape[0] % (bsz * n_seq) == 0 and w_cast2.shape[0] == w_cast.shape[0]
    cast_map = lambda b, s: (b * n_seq + s, 0, 0)
    cast_spec = pl.BlockSpec((w_cast.shape[0] // (bsz * n_seq),) + w_cast.shape[1:], cast_map)
    cast2_spec = pl.BlockSpec((w_cast2.shape[0] // (bsz * n_seq),) + w_cast2.shape[1:], cast_map)
    kern = functools.partial(_mix_kernel, ts=ts, aw=aw, alpha=alpha)
    const = lambda b, s: (0, 0)
    resident = pl.Buffered(1)
    return pl.pallas_call(
        kern,
        grid=(bsz, seq // ts),
        in_specs=[
            pl.BlockSpec(memory_space=pltpu.SMEM),
            pl.BlockSpec((None, ts, d), lambda b, s: (b, s, 0)),
            pl.BlockSpec((None, 6, d), lambda b, s: (b, 0, 0)),
            pl.BlockSpec(w_in.shape, const, pipeline_mode=resident),
            pl.BlockSpec(conv_w.shape, const),
            pl.BlockSpec(conv_p.shape, const),
            pl.BlockSpec(w_o.shape, const, pipeline_mode=resident),
            pl.BlockSpec(ln.shape, const),
            cast_spec,
            cast2_spec,
        ],
        out_specs=[pl.BlockSpec((None, ts, d), lambda b, s: (b, s, 0)), cast_spec, cast2_spec],
        out_shape=[jax.ShapeDtypeStruct((bsz, seq, d), F32), jax.ShapeDtypeStruct(w_cast.shape, BF16),
                   jax.ShapeDtypeStruct(w_cast2.shape, BF16)],
        scratch_shapes=[
            pltpu.VMEM((ts, aw), BF16),
            pltpu.VMEM((4, ts + WINDOW, LANES), BF16),
            pltpu.VMEM((4, ts + WINDOW, LANES), BF16),
            pltpu.VMEM((ts + CONV_HIST, cw), F32),
            pltpu.VMEM((SUBLANES - 1, ts + CONV_HIST - SUBLANES, cw), F32),
            pltpu.VMEM((ts, d), BF16),
        ],
        compiler_params=pltpu.CompilerParams(
            dimension_semantics=("arbitrary", "arbitrary"), vmem_limit_bytes=VMEM_LIMIT),
        name="mix",
    )(sinks, x, mod, w_in, conv_w, conv_p, w_o, ln, w_cast, w_cast2)


def _ffn_pre_kernel(x_ref, mod_ref, wgs_ref, wus_ref, wds_ref, wrt_ref, rb_ref, wca_ref,
                    h_ref, sh_ref, ei_ref, ew_ref, cnt_ref, wca_out_ref):
    wca_out_ref[...] = wca_ref[...].astype(BF16)
    tm = x_ref.shape[0]
    n_exp = wrt_ref.shape[0]
    per = n_exp // N_EXPERT_GROUPS
    hf = x_ref[...] * (1.0 + mod_ref[4:5, :]) + mod_ref[3:4, :]
    _to_slab(h_ref, hf)
    h = hf.astype(BF16)

    gate = jnp.dot(h, wgs_ref[...], preferred_element_type=F32)
    up = jnp.dot(h, wus_ref[...], preferred_element_type=F32)
    act = (gate * jax.nn.sigmoid(gate) * up).astype(BF16)
    sh_ref[...] = jnp.dot(act, wds_ref[...], preferred_element_type=F32)

    logits = lax.dot_general(wrt_ref[...], h, (((1,), (1,)), ((), ())), preferred_element_type=F32)
    scores = jax.nn.sigmoid(logits)
    sel = scores + rb_ref[...]

    iota_p = lax.broadcasted_iota(jnp.int32, (per, tm), 0).astype(F32)
    gs_rows = []
    for g in range(N_EXPERT_GROUPS):
        blk = sel[g * per:(g + 1) * per, :]
        m1 = jnp.max(blk, axis=0, keepdims=True)
        i1 = jnp.min(jnp.where(blk == m1, iota_p, float(per)), axis=0, keepdims=True)
        m2 = jnp.max(jnp.where(iota_p == i1, NEG_INF, blk), axis=0, keepdims=True)
        gs_rows.append(m1 + m2)
    gs = jnp.concatenate(gs_rows, axis=0)
    iota_g = lax.broadcasted_iota(jnp.int32, (N_EXPERT_GROUPS, tm), 0).astype(F32)
    gmask = jnp.zeros((N_EXPERT_GROUPS, tm), jnp.bool_)
    for _ in range(TOPK_EXPERT_GROUPS):
        m = jnp.max(gs, axis=0, keepdims=True)
        gi = jnp.min(jnp.where(gs == m, iota_g, float(N_EXPERT_GROUPS)), axis=0, keepdims=True)
        hit = iota_g == gi
        gmask = gmask | hit
        gs = jnp.where(hit, NEG_INF, gs)
    emask = jnp.concatenate(
        [jnp.broadcast_to(gmask[g:g + 1, :], (per, tm)) for g in range(N_EXPERT_GROUPS)], axis=0)
    cand = jnp.where(emask, sel, NEG_INF)

    iota_e = lax.broadcasted_iota(jnp.int32, (n_exp, tm), 0).astype(F32)
    idx_rows, w_rows = [], []
    for _ in range(TOP_K):
        m = jnp.max(cand, axis=0, keepdims=True)
        ei = jnp.min(jnp.where(cand == m, iota_e, float(n_exp)), axis=0, keepdims=True)
        hit = iota_e == ei
        w_rows.append(jnp.sum(jnp.where(hit, scores, 0.0), axis=0, keepdims=True))
        idx_rows.append(ei)
        cand = jnp.where(hit, NEG_INF, cand)
    wk = jnp.concatenate(w_rows, axis=0)
    ew_ref[...] = wk / jnp.sum(wk, axis=0, keepdims=True) * ROUTED_SCALE
    ei_ref[...] = jnp.concatenate(idx_rows, axis=0).astype(jnp.int32)
    chosen = jnp.where(emask & (cand == NEG_INF), 1.0, 0.0).astype(BF16)
    cnt_ref[...] = jnp.dot(chosen, jnp.ones((tm, LANES), BF16), preferred_element_type=F32)


def _ffn_pre(x1, mod, wgs, wus, wds, wrt, rbias, w_cast, seq):
    n_tok, d = x1.shape
    tm = min(TOK_TILE, seq)
    n_steps = n_tok // tm
    assert seq % tm == 0 and n_tok % tm == 0 and d == SUBLANES * LANES
    assert w_cast.shape[0] % n_steps == 0
    cast_blk = (w_cast.shape[0] // n_steps,) + w_cast.shape[1:]
    cast_spec = pl.BlockSpec(cast_blk, lambda i: (i, 0, 0))
    n_exp = wrt.shape[0]
    const = lambda i: (0, 0)
    return pl.pallas_call(
        _ffn_pre_kernel,
        grid=(n_tok // tm,),
        in_specs=[
            pl.BlockSpec((tm, d), lambda i: (i, 0)),
            pl.BlockSpec((None, 6, d), lambda i: ((i * tm) // seq, 0, 0)),
            pl.BlockSpec(wgs.shape, const),
            pl.BlockSpec(wus.shape, const),
            pl.BlockSpec(wds.shape, const),
            pl.BlockSpec(wrt.shape, const),
            pl.BlockSpec(rbias.shape, const),
            cast_spec,
        ],
        out_specs=[
            pl.BlockSpec((tm * SUBLANES, LANES), lambda i: (i, 0)),
            pl.BlockSpec((tm, d), lambda i: (i, 0)),
            pl.BlockSpec((TOP_K, tm), lambda i: (0, i)),
            pl.BlockSpec((TOP_K, tm), lambda i: (0, i)),
            pl.BlockSpec((None, n_exp, LANES), lambda i: (i, 0, 0)),
            cast_spec,
        ],
        out_shape=[
            jax.ShapeDtypeStruct((n_tok * SUBLANES, LANES), F32),
            jax.ShapeDtypeStruct((n_tok, d), F32),
            jax.ShapeDtypeStruct((TOP_K, n_tok), jnp.int32),
            jax.ShapeDtypeStruct((TOP_K, n_tok), F32),
            jax.ShapeDtypeStruct((n_tok // tm, n_exp, LANES), F32),
            jax.ShapeDtypeStruct(w_cast.shape, BF16),
        ],
        compiler_params=pltpu.CompilerParams(
            dimension_semantics=("arbitrary",), vmem_limit_bytes=VMEM_LIMIT),
        name="ffn_pre",
    )(x1, mod, wgs, wus, wds, wrt, rbias, w_cast)


def _moe_kernel(off_ref, cnt_ref, tok_ref, w_ref, h_hbm, wg_ref, wu_ref, wd_ref, out_hbm,
                h_vmem, acc_ref, *rest, tt):
    j = pl.program_id(0)
    i = pl.program_id(1)
    n_grp = pl.num_programs(1)
    n_exp = n_grp * MOE_EXPERTS_PER_STEP
    eb = MOE_EXPERTS_PER_STEP
    xs, xd_ref, ys, yd_ref, sem = rest[:eb], rest[eb], rest[eb + 1:2 * eb + 1], rest[2 * eb + 1], rest[2 * eb + 2]
    rows = xd_ref.shape[0] // SUBLANES
    seg = MOE_SEG
    last = pl.num_programs(0) * n_exp - 1
    row_iota = lax.broadcasted_iota(jnp.int32, (rows, 1), 0)

    def slab(r):
        if isinstance(r, int):
            return pl.ds(r * SUBLANES, SUBLANES)
        return pl.ds(pl.multiple_of(r * SUBLANES, SUBLANES), SUBLANES)

    def gather_rows(x_ref, base, row_ids):
        for r in row_ids:
            x_ref[slab(r), :] = h_vmem[pl.ds(pl.multiple_of(tok_ref[0, 0, base + r], SUBLANES), SUBLANES), :]

    def scatter_rows(y_ref, base, row_ids):
        for g in range(0, len(row_ids), RMW_UNROLL):
            updates = []
            for r in row_ids[g:g + RMW_UNROLL]:
                dst = pl.ds(pl.multiple_of(tok_ref[0, 0, base + r], SUBLANES), SUBLANES)
                updates.append((dst, acc_ref[dst, :] + w_ref[0, 0, base + r] * y_ref[slab(r), :]))
            for dst, val in reversed(updates):
                acc_ref[dst, :] = val

    def expert_mlp(x_ref, y_ref, k, n_valid):
        x = _from_slab(x_ref, rows).astype(BF16)
        gate = jnp.dot(x, wg_ref[k], preferred_element_type=F32)
        up = jnp.dot(x, wu_ref[k], preferred_element_type=F32)
        act = (gate * jax.nn.sigmoid(gate) * up).astype(BF16)
        y = jnp.dot(act, wd_ref[k], preferred_element_type=F32)
        _to_slab(y_ref, jnp.where(row_iota < jnp.minimum(n_valid, seg), y, 0.0))

    def gather_loop(x_ref, base):
        def body(g, carry):
            gather_rows(x_ref, base, [g * SUBLANES + u for u in range(SUBLANES)])
            return carry
        lax.fori_loop(0, seg // SUBLANES, body, 0)

    def scatter_loop(y_ref, base, n_rows):
        def body(g, carry):
            scatter_rows(y_ref, base, [g * RMW_UNROLL + u for u in range(RMW_UNROLL)])
            return carry
        lax.fori_loop(0, (jnp.minimum(n_rows, seg) + RMW_UNROLL - 1) // RMW_UNROLL, body, 0)

    idx0 = j * n_exp + i * MOE_EXPERTS_PER_STEP

    @pl.when(i == 0)
    def _():
        load = pltpu.make_async_copy(h_hbm.at[pl.ds(j * (tt * SUBLANES), tt * SUBLANES)], h_vmem, sem)
        load.start()
        acc_ref[...] = jnp.zeros_like(acc_ref)
        for ref in (*xs, xd_ref, *ys):
            ref[...] = jnp.zeros_like(ref)
        load.wait()
        for k in range(eb):
            gather_loop(xs[k], off_ref[idx0 + k])

    static_rows = list(range(seg))
    for k in range(eb):
        scatter_rows(ys[k], off_ref[jnp.maximum(idx0 + k - eb, 0)], static_rows)
    for k in range(eb):
        expert_mlp(xs[k], ys[k], k, cnt_ref[idx0 + k])
    for k in range(eb):
        gather_rows(xs[k], off_ref[jnp.minimum(idx0 + k + eb, last)], static_rows)

    for k in range(MOE_EXPERTS_PER_STEP):
        off = off_ref[idx0 + k]
        cnt = cnt_ref[idx0 + k]

        def extra(c, carry):
            gather_loop(xd_ref, off + c * seg)
            expert_mlp(xd_ref, yd_ref, k, cnt - c * seg)
            scatter_loop(yd_ref, off + c * seg, cnt - c * seg)
            return carry

        lax.fori_loop(1, (cnt + seg - 1) // seg, extra, 0)

    @pl.when(i == n_grp - 1)
    def _():
        for k in range(eb):
            scatter_loop(ys[k], off_ref[idx0 + k], cnt_ref[idx0 + k])
        store = pltpu.make_async_copy(acc_ref, out_hbm.at[pl.ds(j * (tt * SUBLANES), tt * SUBLANES)], sem)
        store.start()
        store.wait()


def _moe(h_slab, tok_s, w_s, off, cnt, w_gate, w_up, w_down, tt):
    n_tok = h_slab.shape[0] // SUBLANES
    n_exp, d, f = w_gate.shape
    n_super = n_tok // tt
    rows = MOE_ROWS
    eb = MOE_EXPERTS_PER_STEP
    assert rows % (2 * SUBLANES) == 0 and n_exp % eb == 0
    assert MOE_SEG <= rows and MOE_SEG % SUBLANES == 0 and MOE_SEG % RMW_UNROLL == 0
    assert tok_s.shape[2] >= tt * TOP_K + rows
    w_map = lambda j, i, off, cnt: (i, 0, 0)
    lst_map = lambda j, i, off, cnt: (j, 0, 0)
    return pl.pallas_call(
        functools.partial(_moe_kernel, tt=tt),
        grid_spec=pltpu.PrefetchScalarGridSpec(
            num_scalar_prefetch=2,
            grid=(n_super, n_exp // eb),
            in_specs=[
                pl.BlockSpec((1, 1, tok_s.shape[2]), lst_map, memory_space=pltpu.SMEM),
                pl.BlockSpec((1, 1, w_s.shape[2]), lst_map, memory_space=pltpu.SMEM),
                pl.BlockSpec(memory_space=pl.ANY),
                pl.BlockSpec((eb, d, f), w_map),
                pl.BlockSpec((eb, d, f), w_map),
                pl.BlockSpec((eb, f, d), w_map),
            ],
            out_specs=pl.BlockSpec(memory_space=pl.ANY),
            scratch_shapes=[
                pltpu.VMEM((tt * SUBLANES, LANES), F32),
                pltpu.VMEM((tt * SUBLANES, LANES), F32),
                *[pltpu.VMEM((rows * SUBLANES, LANES), F32) for _ in range(2 * eb + 2)],
                pltpu.SemaphoreType.DMA,
            ],
        ),
        out_shape=jax.ShapeDtypeStruct((n_tok * SUBLANES, LANES), F32),
        compiler_params=pltpu.CompilerParams(
            dimension_semantics=("arbitrary", "arbitrary"), vmem_limit_bytes=VMEM_LIMIT),
        name="moe",
    )(off, cnt, tok_s, w_s, h_slab, w_gate, w_up, w_down)


def _dispatch_lists(eidx_t, ew_t, counts, tt):
    n_tok = eidx_t.shape[1]
    n_tiles, n_exp = counts.shape[0], counts.shape[1]
    n_super = n_tok // tt
    key = eidx_t * tt + jnp.arange(n_tok, dtype=jnp.int32) % tt

    def by_super_tile(a):
        return a.reshape(TOP_K, n_super, tt).transpose(1, 0, 2).reshape(n_super, TOP_K * tt)

    key_s, w_s = lax.sort((by_super_tile(key), by_super_tile(ew_t)), dimension=1, num_keys=1)
    tok_s = (key_s % tt) * SUBLANES
    pad = ((0, 0), (0, 0), (0, MOE_LIST_PAD))
    tok_s = jnp.pad(tok_s.reshape(n_super, 1, tt * TOP_K), pad)
    w_s = jnp.pad(w_s.reshape(n_super, 1, tt * TOP_K), pad)
    cnt = counts[:, :, 0].astype(jnp.int32).reshape(n_super, n_tiles // n_super, n_exp).sum(axis=1)
    off = jnp.cumsum(cnt, axis=1) - cnt
    return tok_s, w_s, off.reshape(-1), cnt.reshape(-1)


def _final_kernel(x_ref, sh_ref, ffn_ref, mod_ref, ln_ref, o_ref, *, alpha):
    tm = x_ref.shape[0]
    ffn = sh_ref[...] + _from_slab(ffn_ref, tm)
    z = alpha * x_ref[...] + (1.0 + mod_ref[5:6, :]) * ffn
    o_ref[...] = _layer_norm_rows(z, ln_ref[0:1, :], ln_ref[1:2, :])


def _final(x1, shared, routed_slab, mod, ln, seq, alpha):
    n_tok, d = x1.shape
    tm = min(FIN_TILE, seq)
    assert seq % tm == 0
    return pl.pallas_call(
        functools.partial(_final_kernel, alpha=alpha),
        grid=(n_tok // tm,),
        in_specs=[
            pl.BlockSpec((tm, d), lambda i: (i, 0)),
            pl.BlockSpec((tm, d), lambda i: (i, 0)),
            pl.BlockSpec((tm * SUBLANES, LANES), lambda i: (i, 0)),
            pl.BlockSpec((None, 6, d), lambda i: ((i * tm) // seq, 0, 0)),
            pl.BlockSpec(ln.shape, lambda i: (0, 0)),
        ],
        out_specs=pl.BlockSpec((tm, d), lambda i: (i, 0)),
        out_shape=jax.ShapeDtypeStruct((n_tok, d), F32),
        compiler_params=pltpu.CompilerParams(
            dimension_semantics=("arbitrary",), vmem_limit_bytes=VMEM_LIMIT),
        name="final",
    )(x1, shared, routed_slab, mod, ln)


def kernel(x, c, w_ada, b_ada, w_in, sinks, conv_w, conv_b, conv_ln_g, conv_ln_b, w_o, ln1_g, ln1_b,
           w_router, router_bias, w_gate_e, w_up_e, w_down_e, w_gate_s, w_up_s, w_down_s, ln2_g, ln2_b):
    bsz, seq, d = x.shape
    depth = w_ada.shape[0]
    n_exp = w_router.shape[2]
    n_tok = bsz * seq
    alpha = (2.0 * depth) ** 0.25
    tt = min(SUPER_TILE, n_tok)
    assert n_tok % tt == 0 and tt % min(TOK_TILE, seq) == 0

    for l in range(depth):
        mod = _ada(c, w_ada[l], b_ada[l]).reshape(bsz, 6, d)
        conv_p = jnp.stack([conv_b[l], conv_ln_g[l], conv_ln_b[l]])
        x1, wd_bf, wu_bf = _mix(x, mod, w_in[l].astype(BF16), sinks[l], conv_w[l], conv_p,
                                w_o[l].astype(BF16), jnp.stack([ln1_g[l], ln1_b[l]]),
                                w_down_e[l], w_up_e[l], alpha)
        x1 = x1.reshape(n_tok, d)
        h_slab, shared, eidx_t, ew_t, counts, wg_bf = _ffn_pre(
            x1, mod, w_gate_s[l].astype(BF16), w_up_s[l].astype(BF16), w_down_s[l].astype(BF16),
            w_router[l].T.astype(BF16), router_bias[l].reshape(n_exp, 1), w_gate_e[l], seq)
        tok_s, w_s, off, cnt = _dispatch_lists(eidx_t, ew_t, counts, tt)
        routed = _moe(h_slab, tok_s, w_s, off, cnt, wg_bf, wu_bf, wd_bf, tt)
        x = _final(x1, shared, routed, mod, jnp.stack([ln2_g[l], ln2_b[l]]), seq, alpha)
        x = x.reshape(bsz, seq, d)
    return x
```

```python
import functools
import math

import jax
import jax.numpy as jnp
from jax import lax
from jax.experimental import pallas as pl
from jax.experimental.pallas import tpu as pltpu

F32 = jnp.float32
BF16 = jnp.bfloat16
NEG_INF = float("-inf")

HEAD_DIM = 64
WINDOW = 128
CONV_KERNEL = 31
CONV_HIST = 32
TOP_K = 8
N_EXPERT_GROUPS = 8
TOPK_EXPERT_GROUPS = 4
ROUTED_SCALE = 2.5
LN_EPS = 1e-5

LANES = 128
SUBLANES = 8
SEQ_TILE = 512
TOK_TILE = 512
FIN_TILE = 512
SUPER_TILE = 4096
MOE_ROWS = 192
MOE_SEG = 168
MOE_EXPERTS_PER_STEP = 4
MOE_LIST_PAD = 256
RMW_UNROLL = 8
CONV_ROWS = 64
VMEM_LIMIT = 56 * 1024 * 1024


def _alibi_slopes(n_heads):
    return [2.0 ** (-8.0 * (i + 1) / n_heads) for i in range(n_heads)]


def _layer_norm_rows(z, g, b):
    mu = jnp.mean(z, axis=-1, keepdims=True)
    d = z - mu
    var = jnp.mean(d * d, axis=-1, keepdims=True)
    return d * lax.rsqrt(var + LN_EPS) * g + b


def _to_slab(ref, val):
    n = val.shape[0]
    for s in range(SUBLANES):
        ref[pl.ds(s, n, stride=SUBLANES), :] = val[:, s * LANES:(s + 1) * LANES]


def _from_slab(ref, n):
    return jnp.concatenate([ref[pl.ds(s, n, stride=SUBLANES), :] for s in range(SUBLANES)], axis=1)


def _ada_kernel(c_ref, w_ref, b_ref, o_ref):
    c = c_ref[...]
    ca = (c * jax.nn.sigmoid(c)).astype(BF16)
    o_ref[...] = jnp.dot(ca, w_ref[...].astype(BF16), preferred_element_type=F32) + b_ref[...]


def _ada(c, w_ada, b_ada):
    bsz, d = c.shape
    n_out = w_ada.shape[1]
    return pl.pallas_call(
        _ada_kernel,
        grid=(n_out // d,),
        in_specs=[
            pl.BlockSpec((bsz, d), lambda j: (0, 0)),
            pl.BlockSpec((d, d), lambda j: (0, j)),
            pl.BlockSpec((1, d), lambda j: (0, j)),
        ],
        out_specs=pl.BlockSpec((bsz, d), lambda j: (0, j)),
        out_shape=jax.ShapeDtypeStruct((bsz, n_out), F32),
        compiler_params=pltpu.CompilerParams(vmem_limit_bytes=VMEM_LIMIT),
        name="ada",
    )(c, w_ada, b_ada.reshape(1, n_out))


def _mix_kernel(sinks_ref, x_ref, mod_ref, win_ref, convw_ref, convp_ref, wo_ref, ln_ref, wca_ref, wcb_ref,
                o_ref, wca_out_ref, wcb_out_ref, q_ref, ke_ref, ve_ref, glu_ref, gsh_ref, cat_ref,
                *, ts, aw, alpha):
    wca_out_ref[...] = wca_ref[...].astype(BF16)
    wcb_out_ref[...] = wcb_ref[...].astype(BF16)
    s_idx = pl.program_id(1)
    n_heads = aw // HEAD_DIM
    slopes = _alibi_slopes(n_heads)
    cw = cat_ref.shape[1] - aw

    @pl.when(s_idx == 0)
    def _():
        ke_ref[:, 0:WINDOW, :] = jnp.zeros((4, WINDOW, LANES), BF16)
        ve_ref[:, 0:WINDOW, :] = jnp.zeros((4, WINDOW, LANES), BF16)
        glu_ref[0:CONV_HIST, :] = jnp.zeros((CONV_HIST, cw), F32)

    @pl.when(s_idx > 0)
    def _():
        ke_ref[:, 0:WINDOW, :] = ke_ref[:, ts:ts + WINDOW, :]
        ve_ref[:, 0:WINDOW, :] = ve_ref[:, ts:ts + WINDOW, :]
        glu_ref[0:CONV_HIST, :] = glu_ref[ts:ts + CONV_HIST, :]

    x = x_ref[...]
    h = (x * (1.0 + mod_ref[1:2, :]) + mod_ref[0:1, :]).astype(BF16)

    q = jnp.dot(h, win_ref[:, 0:aw], preferred_element_type=F32)
    q_ref[...] = (q * (1.0 / math.sqrt(HEAD_DIM))).astype(BF16)
    kv = jnp.dot(h, win_ref[:, aw:aw + 2 * LANES], preferred_element_type=F32)
    lo = lax.broadcasted_iota(jnp.int32, (ts, LANES), 1) < HEAD_DIM
    for dst, t in ((ke_ref, kv[:, 0:LANES]), (ve_ref, kv[:, LANES:2 * LANES])):
        t_r = pltpu.roll(t, HEAD_DIM, axis=1)
        dst[0, WINDOW:WINDOW + ts, :] = jnp.where(lo, t, 0.0).astype(BF16)
        dst[1, WINDOW:WINDOW + ts, :] = jnp.where(lo, 0.0, t_r).astype(BF16)
        dst[2, WINDOW:WINDOW + ts, :] = jnp.where(lo, t_r, 0.0).astype(BF16)
        dst[3, WINDOW:WINDOW + ts, :] = jnp.where(lo, 0.0, t).astype(BF16)
    u0 = aw + 2 * LANES
    ga = jnp.dot(h, win_ref[:, u0:u0 + cw], preferred_element_type=F32)
    gb = jnp.dot(h, win_ref[:, u0 + cw:u0 + 2 * cw], preferred_element_type=F32)
    glu_ref[CONV_HIST:CONV_HIST + ts, :] = ga * jax.nn.sigmoid(gb)

    qi = lax.broadcasted_iota(jnp.int32, (WINDOW, 2 * WINDOW), 0)
    kj = lax.broadcasted_iota(jnp.int32, (WINDOW, 2 * WINDOW), 1)
    dist = WINDOW + qi - kj
    band = (dist >= 0) & (dist < WINDOW)
    neg_dist = -dist.astype(F32)
    bias_any = jnp.where(band, neg_dist, NEG_INF)
    bias_first = jnp.where(band & ((kj >= WINDOW) | (s_idx > 0)), neg_dist, NEG_INF)
    for i in range(ts // WINDOW):
        r0 = i * WINDOW
        bias = bias_first if i == 0 else bias_any
        for pair in range(aw // LANES):
            g = (2 * pair) // (n_heads // 2)
            qp = q_ref[r0:r0 + WINDOW, pair * LANES:(pair + 1) * LANES]
            out_pair = None
            for par in range(2):
                hd = 2 * pair + par
                kk = ke_ref[2 * g + par, r0:r0 + 2 * WINDOW, :]
                s = lax.dot_general(qp, kk, (((1,), (1,)), ((), ())), preferred_element_type=F32)
                s = s + slopes[hd] * bias
                sink = sinks_ref[hd]
                m = jnp.maximum(jnp.max(s, axis=-1, keepdims=True), sink)
                p = jnp.exp(s - m)
                denom = jnp.sum(p, axis=-1, keepdims=True) + jnp.exp(sink - m)
                vv = ve_ref[2 * g + par, r0:r0 + 2 * WINDOW, :]
                o = jnp.dot(p.astype(BF16), vv, preferred_element_type=F32) * (1.0 / denom)
                out_pair = o if out_pair is None else out_pair + o
            cat_ref[r0:r0 + WINDOW, pair * LANES:(pair + 1) * LANES] = out_pair.astype(BF16)

    conv_b = convp_ref[0:1, :]
    cln_g = convp_ref[1:2, :]
    cln_b = convp_ref[2:3, :]
    off = CONV_HIST - (CONV_KERNEL - 1)
    n_sh = gsh_ref.shape[1]
    for p in range(1, SUBLANES):
        gsh_ref[p - 1] = glu_ref[p:p + n_sh, :]
    for c in range(ts // CONV_ROWS):
        c0 = c * CONV_ROWS
        acc = jnp.broadcast_to(conv_b, (CONV_ROWS, cw))
        for j in range(CONV_KERNEL):
            a, p = divmod(off + j, SUBLANES)
            r0 = c0 + a * SUBLANES
            tap = glu_ref[r0:r0 + CONV_ROWS, :] if p == 0 else gsh_ref[p - 1, r0:r0 + CONV_ROWS, :]
            acc = acc + tap * convw_ref[j:j + 1, :]
        yn = _layer_norm_rows(acc, cln_g, cln_b)
        cat_ref[c0:c0 + CONV_ROWS, aw:aw + cw] = (yn * jax.nn.sigmoid(yn)).astype(BF16)

    mix = jnp.dot(cat_ref[...], wo_ref[...], preferred_element_type=F32)
    z = alpha * x + (1.0 + mod_ref[2:3, :]) * mix
    o_ref[...] = _layer_norm_rows(z, ln_ref[0:1, :], ln_ref[1:2, :])


def _mix(x, mod, w_in, sinks, conv_w, conv_p, w_o, ln, w_cast, w_cast2, alpha):
    bsz, seq, d = x.shape
    cw = conv_w.shape[1]
    aw = d - cw
    ts = min(SEQ_TILE, seq)
    n_seq = seq // ts
    assert seq % ts == 0 and ts % WINDOW == 0 and aw % LANES == 0
    assert (aw // HEAD_DIM) // 4 == 2, "kernel packs exactly two KV heads into one lane group"
    assert w_in.shape[1] == aw + 2 * LANES + 2 * cw
    assert w_cast.shape[0] % (bsz * n_seq) == 0 and w_cast2.shape[0] == w_cast.shape[0]
    cast_map = lambda b, s: (b * n_seq + s, 0, 0)
    cast_spec = pl.BlockSpec((w_cast.shape[0] // (bsz * n_seq),) + w_cast.shape[1:], cast_map)
    cast2_spec = pl.BlockSpec((w_cast2.shape[0] // (bsz * n_seq),) + w_cast2.shape[1:], cast_map)
    kern = functools.partial(_mix_kernel, ts=ts, aw=aw, alpha=alpha)
    const = lambda b, s: (0, 0)
    resident = pl.Buffered(1)
    return pl.pallas_call(
        kern,
        grid=(bsz, seq // ts),
        in_specs=[
            pl.BlockSpec(memory_space=pltpu.SMEM),
            pl.BlockSpec((None, ts, d), lambda b, s: (b, s, 0)),
            pl.BlockSpec((None, 6, d), lambda b, s: (b, 0, 0)),
            pl.BlockSpec(w_in.shape, const, pipeline_mode=resident),
            pl.BlockSpec(conv_w.shape, const),
            pl.BlockSpec(conv_p.shape, const),
            pl.BlockSpec(w_o.shape, const, pipeline_mode=resident),
            pl.BlockSpec(ln.shape, const),
            cast_spec,
            cast2_spec,
        ],
        out_specs=[pl.BlockSpec((None, ts, d), lambda b, s: (b, s, 0)), cast_spec, cast2_spec],
        out_shape=[jax.ShapeDtypeStruct((bsz, seq, d), F32), jax.ShapeDtypeStruct(w_cast.shape, BF16),
                   jax.ShapeDtypeStruct(w_cast2.shape, BF16)],
        scratch_shapes=[
            pltpu.VMEM((ts, aw), BF16),
            pltpu.VMEM((4, ts + WINDOW, LANES), BF16),
            pltpu.VMEM((4, ts + WINDOW, LANES), BF16),
            pltpu.VMEM((ts + CONV_HIST, cw), F32),
            pltpu.VMEM((SUBLANES - 1, ts + CONV_HIST - SUBLANES, cw), F32),
            pltpu.VMEM((ts, d), BF16),
        ],
        compiler_params=pltpu.CompilerParams(
            dimension_semantics=("arbitrary", "arbitrary"), vmem_limit_bytes=VMEM_LIMIT),
        name="mix",
    )(sinks, x, mod, w_in, conv_w, conv_p, w_o, ln, w_cast, w_cast2)


def _ffn_pre_kernel(x_ref, mod_ref, wgs_ref, wus_ref, wds_ref, wrt_ref, rb_ref, wca_ref,
                    h_ref, sh_ref, ei_ref, ew_ref, cnt_ref, wca_out_ref):
    wca_out_ref[...] = wca_ref[...].astype(BF16)
    tm = x_ref.shape[0]
    n_exp = wrt_ref.shape[0]
    per = n_exp // N_EXPERT_GROUPS
    hf = x_ref[...] * (1.0 + mod_ref[4:5, :]) + mod_ref[3:4, :]
    _to_slab(h_ref, hf)
    h = hf.astype(BF16)

    gate = jnp.dot(h, wgs_ref[...], preferred_element_type=F32)
    up = jnp.dot(h, wus_ref[...], preferred_element_type=F32)
    act = (gate * jax.nn.sigmoid(gate) * up).astype(BF16)
    sh_ref[...] = jnp.dot(act, wds_ref[...], preferred_element_type=F32)

    logits = lax.dot_general(wrt_ref[...], h, (((1,), (1,)), ((), ())), preferred_element_type=F32)
    scores = jax.nn.sigmoid(logits)
    sel = scores + rb_ref[...]

    iota_p = lax.broadcasted_iota(jnp.int32, (per, tm), 0).astype(F32)
    gs_rows = []
    for g in range(N_EXPERT_GROUPS):
        blk = sel[g * per:(g + 1) * per, :]
        m1 = jnp.max(blk, axis=0, keepdims=True)
        i1 = jnp.min(jnp.where(blk == m1, iota_p, float(per)), axis=0, keepdims=True)
        m2 = jnp.max(jnp.where(iota_p == i1, NEG_INF, blk), axis=0, keepdims=True)
        gs_rows.append(m1 + m2)
    gs = jnp.concatenate(gs_rows, axis=0)
    iota_g = lax.broadcasted_iota(jnp.int32, (N_EXPERT_GROUPS, tm), 0).astype(F32)
    gmask = jnp.zeros((N_EXPERT_GROUPS, tm), jnp.bool_)
    for _ in range(TOPK_EXPERT_GROUPS):
        m = jnp.max(gs, axis=0, keepdims=True)
        gi = jnp.min(jnp.where(gs == m, iota_g, float(N_EXPERT_GROUPS)), axis=0, keepdims=True)
        hit = iota_g == gi
        gmask = gmask | hit
        gs = jnp.where(hit, NEG_INF, gs)
    emask = jnp.concatenate(
        [jnp.broadcast_to(gmask[g:g + 1, :], (per, tm)) for g in range(N_EXPERT_GROUPS)], axis=0)
    cand = jnp.where(emask, sel, NEG_INF)

    iota_e = lax.broadcasted_iota(jnp.int32, (n_exp, tm), 0).astype(F32)
    idx_rows, w_rows = [], []
    for _ in range(TOP_K):
        m = jnp.max(cand, axis=0, keepdims=True)
        ei = jnp.min(jnp.where(cand == m, iota_e, float(n_exp)), axis=0, keepdims=True)
        hit = iota_e == ei
        w_rows.append(jnp.sum(jnp.where(hit, scores, 0.0), axis=0, keepdims=True))
        idx_rows.append(ei)
        cand = jnp.where(hit, NEG_INF, cand)
    wk = jnp.concatenate(w_rows, axis=0)
    ew_ref[...] = wk / jnp.sum(wk, axis=0, keepdims=True) * ROUTED_SCALE
    ei_ref[...] = jnp.concatenate(idx_rows, axis=0).astype(jnp.int32)
    chosen = jnp.where(emask & (cand == NEG_INF), 1.0, 0.0).astype(BF16)
    cnt_ref[...] = jnp.dot(chosen, jnp.ones((tm, LANES), BF16), preferred_element_type=F32)


def _ffn_pre(x1, mod, wgs, wus, wds, wrt, rbias, w_cast, seq):
    n_tok, d = x1.shape
    tm = min(TOK_TILE, seq)
    n_steps = n_tok // tm
    assert seq % tm == 0 and n_tok % tm == 0 and d == SUBLANES * LANES
    assert w_cast.shape[0] % n_steps == 0
    cast_blk = (w_cast.shape[0] // n_steps,) + w_cast.shape[1:]
    cast_spec = pl.BlockSpec(cast_blk, lambda i: (i, 0, 0))
    n_exp = wrt.shape[0]
    const = lambda i: (0, 0)
    return pl.pallas_call(
        _ffn_pre_kernel,
        grid=(n_tok // tm,),
        in_specs=[
            pl.BlockSpec((tm, d), lambda i: (i, 0)),
            pl.BlockSpec((None, 6, d), lambda i: ((i * tm) // seq, 0, 0)),
            pl.BlockSpec(wgs.shape, const),
            pl.BlockSpec(wus.shape, const),
            pl.BlockSpec(wds.shape, const),
            pl.BlockSpec(wrt.shape, const),
            pl.BlockSpec(rbias.shape, const),
            cast_spec,
        ],
        out_specs=[
            pl.BlockSpec((tm * SUBLANES, LANES), lambda i: (i, 0)),
            pl.BlockSpec((tm, d), lambda i: (i, 0)),
            pl.BlockSpec((TOP_K, tm), lambda i: (0, i)),
            pl.BlockSpec((TOP_K, tm), lambda i: (0, i)),
            pl.BlockSpec((None, n_exp, LANES), lambda i: (i, 0, 0)),
            cast_spec,
        ],
        out_shape=[
            jax.ShapeDtypeStruct((n_tok * SUBLANES, LANES), F32),
            jax.ShapeDtypeStruct((n_tok, d), F32),
            jax.ShapeDtypeStruct((TOP_K, n_tok), jnp.int32),
            jax.ShapeDtypeStruct((TOP_K, n_tok), F32),
            jax.ShapeDtypeStruct((n_tok // tm, n_exp, LANES), F32),
            jax.ShapeDtypeStruct(w_cast.shape, BF16),
        ],
        compiler_params=pltpu.CompilerParams(
            dimension_semantics=("arbitrary",), vmem_limit_bytes=VMEM_LIMIT),
        name="ffn_pre",
    )(x1, mod, wgs, wus, wds, wrt, rbias, w_cast)


def _moe_kernel(off_ref, cnt_ref, tok_ref, w_ref, h_hbm, wg_ref, wu_ref, wd_ref, out_hbm,
                h_vmem, acc_ref, *rest, tt):
    j = pl.program_id(0)
    i = pl.program_id(1)
    n_grp = pl.num_programs(1)
    n_exp = n_grp * MOE_EXPERTS_PER_STEP
    eb = MOE_EXPERTS_PER_STEP
    xs, xd_ref, ys, yd_ref, sem = rest[:eb], rest[eb], rest[eb + 1:2 * eb + 1], rest[2 * eb + 1], rest[2 * eb + 2]
    rows = xd_ref.shape[0] // SUBLANES
    seg = MOE_SEG
    last = pl.num_programs(0) * n_exp - 1
    row_iota = lax.broadcasted_iota(jnp.int32, (rows, 1), 0)

    def slab(r):
        if isinstance(r, int):
            return pl.ds(r * SUBLANES, SUBLANES)
        return pl.ds(pl.multiple_of(r * SUBLANES, SUBLANES), SUBLANES)

    def gather_rows(x_ref, base, row_ids):
        for r in row_ids:
            x_ref[slab(r), :] = h_vmem[pl.ds(pl.multiple_of(tok_ref[0, 0, base + r], SUBLANES), SUBLANES), :]

    def scatter_rows(y_ref, base, row_ids):
        for g in range(0, len(row_ids), RMW_UNROLL):
            updates = []
            for r in row_ids[g:g + RMW_UNROLL]:
                dst = pl.ds(pl.multiple_of(tok_ref[0, 0, base + r], SUBLANES), SUBLANES)
                updates.append((dst, acc_ref[dst, :] + w_ref[0, 0, base + r] * y_ref[slab(r), :]))
            for dst, val in reversed(updates):
                acc_ref[dst, :] = val

    def expert_mlp(x_ref, y_ref, k, n_valid):
        x = _from_slab(x_ref, rows).astype(BF16)
        gate = jnp.dot(x, wg_ref[k], preferred_element_type=F32)
        up = jnp.dot(x, wu_ref[k], preferred_element_type=F32)
        act = (gate * jax.nn.sigmoid(gate) * up).astype(BF16)
        y = jnp.dot(act, wd_ref[k], preferred_element_type=F32)
        _to_slab(y_ref, jnp.where(row_iota < jnp.minimum(n_valid, seg), y, 0.0))

    def gather_loop(x_ref, base):
        def body(g, carry):
            gather_rows(x_ref, base, [g * SUBLANES + u for u in range(SUBLANES)])
            return carry
        lax.fori_loop(0, seg // SUBLANES, body, 0)

    def scatter_loop(y_ref, base, n_rows):
        def body(g, carry):
            scatter_rows(y_ref, base, [g * RMW_UNROLL + u for u in range(RMW_UNROLL)])
            return carry
        lax.fori_loop(0, (jnp.minimum(n_rows, seg) + RMW_UNROLL - 1) // RMW_UNROLL, body, 0)

    idx0 = j * n_exp + i * MOE_EXPERTS_PER_STEP

    n_super = pl.num_programs(0)
    slab_rows = tt * SUBLANES

    def h_copy(jj):
        return pltpu.make_async_copy(h_hbm.at[pl.ds(jj * slab_rows, slab_rows)], h_vmem, sem.at[0])

    def out_copy(jj):
        return pltpu.make_async_copy(acc_ref, out_hbm.at[pl.ds(jj * slab_rows, slab_rows)], sem.at[1])

    @pl.when(i == 0)
    def _():
        @pl.when(j == 0)
        def _():
            h_copy(j).start()

        h_copy(j).wait()
        for ref in (*xs, xd_ref, *ys):
            ref[...] = jnp.zeros_like(ref)
        for k in range(eb):
            gather_loop(xs[k], off_ref[idx0 + k])

        @pl.when(j > 0)
        def _():
            out_copy(j - 1).wait()

        acc_ref[...] = jnp.zeros_like(acc_ref)

    static_rows = list(range(seg))
    for k in range(eb):
        scatter_rows(ys[k], off_ref[jnp.maximum(idx0 + k - eb, 0)], static_rows)
    for k in range(eb):
        expert_mlp(xs[k], ys[k], k, cnt_ref[idx0 + k])
    for k in range(eb):
        gather_rows(xs[k], off_ref[jnp.minimum(idx0 + k + eb, last)], static_rows)

    for k in range(MOE_EXPERTS_PER_STEP):
        off = off_ref[idx0 + k]
        cnt = cnt_ref[idx0 + k]

        def extra(c, carry):
            gather_loop(xd_ref, off + c * seg)
            expert_mlp(xd_ref, yd_ref, k, cnt - c * seg)
            scatter_loop(yd_ref, off + c * seg, cnt - c * seg)
            return carry

        lax.fori_loop(1, (cnt + seg - 1) // seg, extra, 0)

    @pl.when(i == n_grp - 1)
    def _():
        @pl.when(j + 1 < n_super)
        def _():
            h_copy(j + 1).start()

        for k in range(eb):
            scatter_loop(ys[k], off_ref[idx0 + k], cnt_ref[idx0 + k])
        out_copy(j).start()

        @pl.when(j + 1 == n_super)
        def _():
            out_copy(j).wait()


def _moe(h_slab, tok_s, w_s, off, cnt, w_gate, w_up, w_down, tt):
    n_tok = h_slab.shape[0] // SUBLANES
    n_exp, d, f = w_gate.shape
    n_super = n_tok // tt
    rows = MOE_ROWS
    eb = MOE_EXPERTS_PER_STEP
    assert rows % (2 * SUBLANES) == 0 and n_exp % eb == 0
    assert MOE_SEG <= rows and MOE_SEG % SUBLANES == 0 and MOE_SEG % RMW_UNROLL == 0
    assert tok_s.shape[2] >= tt * TOP_K + rows
    w_map = lambda j, i, off, cnt: (i, 0, 0)
    lst_map = lambda j, i, off, cnt: (j, 0, 0)
    return pl.pallas_call(
        functools.partial(_moe_kernel, tt=tt),
        grid_spec=pltpu.PrefetchScalarGridSpec(
            num_scalar_prefetch=2,
            grid=(n_super, n_exp // eb),
            in_specs=[
                pl.BlockSpec((1, 1, tok_s.shape[2]), lst_map, memory_space=pltpu.SMEM),
                pl.BlockSpec((1, 1, w_s.shape[2]), lst_map, memory_space=pltpu.SMEM),
                pl.BlockSpec(memory_space=pl.ANY),
                pl.BlockSpec((eb, d, f), w_map),
                pl.BlockSpec((eb, d, f), w_map),
                pl.BlockSpec((eb, f, d), w_map),
            ],
            out_specs=pl.BlockSpec(memory_space=pl.ANY),
            scratch_shapes=[
                pltpu.VMEM((tt * SUBLANES, LANES), F32),
                pltpu.VMEM((tt * SUBLANES, LANES), F32),
                *[pltpu.VMEM((rows * SUBLANES, LANES), F32) for _ in range(2 * eb + 2)],
                pltpu.SemaphoreType.DMA((2,)),
            ],
        ),
        out_shape=jax.ShapeDtypeStruct((n_tok * SUBLANES, LANES), F32),
        compiler_params=pltpu.CompilerParams(
            dimension_semantics=("arbitrary", "arbitrary"), vmem_limit_bytes=VMEM_LIMIT),
        name="moe",
    )(off, cnt, tok_s, w_s, h_slab, w_gate, w_up, w_down)


def _dispatch_lists(eidx_t, ew_t, counts, tt):
    n_tok = eidx_t.shape[1]
    n_tiles, n_exp = counts.shape[0], counts.shape[1]
    n_super = n_tok // tt
    key = eidx_t * tt + jnp.arange(n_tok, dtype=jnp.int32) % tt

    def by_super_tile(a):
        return a.reshape(TOP_K, n_super, tt).transpose(1, 0, 2).reshape(n_super, TOP_K * tt)

    key_s, w_s = lax.sort((by_super_tile(key), by_super_tile(ew_t)), dimension=1, num_keys=1)
    tok_s = (key_s % tt) * SUBLANES
    pad = ((0, 0), (0, 0), (0, MOE_LIST_PAD))
    tok_s = jnp.pad(tok_s.reshape(n_super, 1, tt * TOP_K), pad)
    w_s = jnp.pad(w_s.reshape(n_super, 1, tt * TOP_K), pad)
    cnt = counts[:, :, 0].astype(jnp.int32).reshape(n_super, n_tiles // n_super, n_exp).sum(axis=1)
    off = jnp.cumsum(cnt, axis=1) - cnt
    return tok_s, w_s, off.reshape(-1), cnt.reshape(-1)


def _final_kernel(x_ref, sh_ref, ffn_ref, mod_ref, ln_ref, o_ref, *, alpha):
    tm = x_ref.shape[0]
    ffn = sh_ref[...] + _from_slab(ffn_ref, tm)
    z = alpha * x_ref[...] + (1.0 + mod_ref[5:6, :]) * ffn
    o_ref[...] = _layer_norm_rows(z, ln_ref[0:1, :], ln_ref[1:2, :])


def _final(x1, shared, routed_slab, mod, ln, seq, alpha):
    n_tok, d = x1.shape
    tm = min(FIN_TILE, seq)
    assert seq % tm == 0
    return pl.pallas_call(
        functools.partial(_final_kernel, alpha=alpha),
        grid=(n_tok // tm,),
        in_specs=[
            pl.BlockSpec((tm, d), lambda i: (i, 0)),
            pl.BlockSpec((tm, d), lambda i: (i, 0)),
            pl.BlockSpec((tm * SUBLANES, LANES), lambda i: (i, 0)),
            pl.BlockSpec((None, 6, d), lambda i: ((i * tm) // seq, 0, 0)),
            pl.BlockSpec(ln.shape, lambda i: (0, 0)),
        ],
        out_specs=pl.BlockSpec((tm, d), lambda i: (i, 0)),
        out_shape=jax.ShapeDtypeStruct((n_tok, d), F32),
        compiler_params=pltpu.CompilerParams(
            dimension_semantics=("arbitrary",), vmem_limit_bytes=VMEM_LIMIT),
        name="final",
    )(x1, shared, routed_slab, mod, ln)


def kernel(x, c, w_ada, b_ada, w_in, sinks, conv_w, conv_b, conv_ln_g, conv_ln_b, w_o, ln1_g, ln1_b,
           w_router, router_bias, w_gate_e, w_up_e, w_down_e, w_gate_s, w_up_s, w_down_s, ln2_g, ln2_b):
    bsz, seq, d = x.shape
    depth = w_ada.shape[0]
    n_exp = w_router.shape[2]
    n_tok = bsz * seq
    alpha = (2.0 * depth) ** 0.25
    tt = min(SUPER_TILE, n_tok)
    assert n_tok % tt == 0 and tt % min(TOK_TILE, seq) == 0

    for l in range(depth):
        mod = _ada(c, w_ada[l], b_ada[l]).reshape(bsz, 6, d)
        conv_p = jnp.stack([conv_b[l], conv_ln_g[l], conv_ln_b[l]])
        x1, wd_bf, wu_bf = _mix(x, mod, w_in[l].astype(BF16), sinks[l], conv_w[l], conv_p,
                                w_o[l].astype(BF16), jnp.stack([ln1_g[l], ln1_b[l]]),
                                w_down_e[l], w_up_e[l], alpha)
        x1 = x1.reshape(n_tok, d)
        h_slab, shared, eidx_t, ew_t, counts, wg_bf = _ffn_pre(
            x1, mod, w_gate_s[l].astype(BF16), w_up_s[l].astype(BF16), w_down_s[l].astype(BF16),
            w_router[l].T.astype(BF16), router_bias[l].reshape(n_exp, 1), w_gate_e[l], seq)
        tok_s, w_s, off, cnt = _dispatch_lists(eidx_t, ew_t, counts, tt)
        routed = _moe(h_slab, tok_s, w_s, off, cnt, wg_bf, wu_bf, wd_bf, tt)
        x = _final(x1, shared, routed, mod, jnp.stack([ln2_g[l], ln2_b[l]]), seq, alpha)
        x = x.reshape(bsz, seq, d)
    return x
```

```python
import functools
import math

import jax
import jax.numpy as jnp
from jax import lax
from jax.experimental import pallas as pl
from jax.experimental.pallas import tpu as pltpu

F32 = jnp.float32
BF16 = jnp.bfloat16
NEG_INF = float("-inf")

HEAD_DIM = 64
WINDOW = 128
CONV_KERNEL = 31
CONV_HIST = 32
TOP_K = 8
N_EXPERT_GROUPS = 8
TOPK_EXPERT_GROUPS = 4
ROUTED_SCALE = 2.5
LN_EPS = 1e-5

LANES = 128
SUBLANES = 8
SEQ_TILE = 512
TOK_TILE = 512
FIN_TILE = 512
SUPER_TILE = 4096
MOE_ROWS = 176
MOE_SEG = 168
MOE_EXPERTS_PER_STEP = 4
MOE_LIST_PAD = 256
RMW_UNROLL = 8
CONV_ROWS = 64
VMEM_LIMIT = 56 * 1024 * 1024


def _alibi_slopes(n_heads):
    return [2.0 ** (-8.0 * (i + 1) / n_heads) for i in range(n_heads)]


def _layer_norm_rows(z, g, b):
    mu = jnp.mean(z, axis=-1, keepdims=True)
    d = z - mu
    var = jnp.mean(d * d, axis=-1, keepdims=True)
    return d * lax.rsqrt(var + LN_EPS) * g + b


def _to_slab(ref, val):
    n = val.shape[0]
    for s in range(SUBLANES):
        ref[pl.ds(s, n, stride=SUBLANES), :] = val[:, s * LANES:(s + 1) * LANES]


def _from_slab(ref, n):
    return jnp.concatenate([ref[pl.ds(s, n, stride=SUBLANES), :] for s in range(SUBLANES)], axis=1)


def _ada_kernel(c_ref, w_ref, b_ref, o_ref):
    c = c_ref[...]
    ca = (c * jax.nn.sigmoid(c)).astype(BF16)
    o_ref[...] = jnp.dot(ca, w_ref[...].astype(BF16), preferred_element_type=F32) + b_ref[...]


def _ada(c, w_ada, b_ada):
    bsz, d = c.shape
    n_out = w_ada.shape[1]
    return pl.pallas_call(
        _ada_kernel,
        grid=(n_out // d,),
        in_specs=[
            pl.BlockSpec((bsz, d), lambda j: (0, 0)),
            pl.BlockSpec((d, d), lambda j: (0, j)),
            pl.BlockSpec((1, d), lambda j: (0, j)),
        ],
        out_specs=pl.BlockSpec((bsz, d), lambda j: (0, j)),
        out_shape=jax.ShapeDtypeStruct((bsz, n_out), F32),
        compiler_params=pltpu.CompilerParams(vmem_limit_bytes=VMEM_LIMIT),
        name="ada",
    )(c, w_ada, b_ada.reshape(1, n_out))


def _mix_kernel(sinks_ref, x_ref, mod_ref, win_ref, convw_ref, convp_ref, wo_ref, ln_ref, wca_ref, wcb_ref,
                o_ref, wca_out_ref, wcb_out_ref, q_ref, ke_ref, ve_ref, glu_ref, gsh_ref, cat_ref,
                *, ts, aw, alpha):
    wca_out_ref[...] = wca_ref[...].astype(BF16)
    wcb_out_ref[...] = wcb_ref[...].astype(BF16)
    s_idx = pl.program_id(1)
    n_heads = aw // HEAD_DIM
    slopes = _alibi_slopes(n_heads)
    cw = cat_ref.shape[1] - aw

    @pl.when(s_idx == 0)
    def _():
        ke_ref[:, 0:WINDOW, :] = jnp.zeros((4, WINDOW, LANES), BF16)
        ve_ref[:, 0:WINDOW, :] = jnp.zeros((4, WINDOW, LANES), BF16)
        glu_ref[0:CONV_HIST, :] = jnp.zeros((CONV_HIST, cw), F32)

    @pl.when(s_idx > 0)
    def _():
        ke_ref[:, 0:WINDOW, :] = ke_ref[:, ts:ts + WINDOW, :]
        ve_ref[:, 0:WINDOW, :] = ve_ref[:, ts:ts + WINDOW, :]
        glu_ref[0:CONV_HIST, :] = glu_ref[ts:ts + CONV_HIST, :]

    x = x_ref[...]
    h = (x * (1.0 + mod_ref[1:2, :]) + mod_ref[0:1, :]).astype(BF16)

    q = jnp.dot(h, win_ref[:, 0:aw], preferred_element_type=F32)
    q_ref[...] = (q * (1.0 / math.sqrt(HEAD_DIM))).astype(BF16)
    kv = jnp.dot(h, win_ref[:, aw:aw + 2 * LANES], preferred_element_type=F32)
    lo = lax.broadcasted_iota(jnp.int32, (ts, LANES), 1) < HEAD_DIM
    for dst, t in ((ke_ref, kv[:, 0:LANES]), (ve_ref, kv[:, LANES:2 * LANES])):
        t_r = pltpu.roll(t, HEAD_DIM, axis=1)
        dst[0, WINDOW:WINDOW + ts, :] = jnp.where(lo, t, 0.0).astype(BF16)
        dst[1, WINDOW:WINDOW + ts, :] = jnp.where(lo, 0.0, t_r).astype(BF16)
        dst[2, WINDOW:WINDOW + ts, :] = jnp.where(lo, t_r, 0.0).astype(BF16)
        dst[3, WINDOW:WINDOW + ts, :] = jnp.where(lo, 0.0, t).astype(BF16)
    u0 = aw + 2 * LANES
    ga = jnp.dot(h, win_ref[:, u0:u0 + cw], preferred_element_type=F32)
    gb = jnp.dot(h, win_ref[:, u0 + cw:u0 + 2 * cw], preferred_element_type=F32)
    glu_ref[CONV_HIST:CONV_HIST + ts, :] = ga * jax.nn.sigmoid(gb)

    qi = lax.broadcasted_iota(jnp.int32, (WINDOW, 2 * WINDOW), 0)
    kj = lax.broadcasted_iota(jnp.int32, (WINDOW, 2 * WINDOW), 1)
    dist = WINDOW + qi - kj
    band = (dist >= 0) & (dist < WINDOW)
    neg_dist = -dist.astype(F32)
    bias_any = jnp.where(band, neg_dist, NEG_INF)
    bias_first = jnp.where(band & ((kj >= WINDOW) | (s_idx > 0)), neg_dist, NEG_INF)
    for i in range(ts // WINDOW):
        r0 = i * WINDOW
        bias = bias_first if i == 0 else bias_any
        for pair in range(aw // LANES):
            g = (2 * pair) // (n_heads // 2)
            qp = q_ref[r0:r0 + WINDOW, pair * LANES:(pair + 1) * LANES]
            out_pair = None
            for par in range(2):
                hd = 2 * pair + par
                kk = ke_ref[2 * g + par, r0:r0 + 2 * WINDOW, :]
                s = lax.dot_general(qp, kk, (((1,), (1,)), ((), ())), preferred_element_type=F32)
                s = s + slopes[hd] * bias
                sink = sinks_ref[hd]
                m = jnp.maximum(jnp.max(s, axis=-1, keepdims=True), sink)
                p = jnp.exp(s - m)
                denom = jnp.sum(p, axis=-1, keepdims=True) + jnp.exp(sink - m)
                vv = ve_ref[2 * g + par, r0:r0 + 2 * WINDOW, :]
                o = jnp.dot(p.astype(BF16), vv, preferred_element_type=F32) * (1.0 / denom)
                out_pair = o if out_pair is None else out_pair + o
            cat_ref[r0:r0 + WINDOW, pair * LANES:(pair + 1) * LANES] = out_pair.astype(BF16)

    conv_b = convp_ref[0:1, :]
    cln_g = convp_ref[1:2, :]
    cln_b = convp_ref[2:3, :]
    off = CONV_HIST - (CONV_KERNEL - 1)
    n_sh = gsh_ref.shape[1]
    for p in range(1, SUBLANES):
        gsh_ref[p - 1] = glu_ref[p:p + n_sh, :]
    for c in range(ts // CONV_ROWS):
        c0 = c * CONV_ROWS
        acc = jnp.broadcast_to(conv_b, (CONV_ROWS, cw))
        for j in range(CONV_KERNEL):
            a, p = divmod(off + j, SUBLANES)
            r0 = c0 + a * SUBLANES
            tap = glu_ref[r0:r0 + CONV_ROWS, :] if p == 0 else gsh_ref[p - 1, r0:r0 + CONV_ROWS, :]
            acc = acc + tap * convw_ref[j:j + 1, :]
        yn = _layer_norm_rows(acc, cln_g, cln_b)
        cat_ref[c0:c0 + CONV_ROWS, aw:aw + cw] = (yn * jax.nn.sigmoid(yn)).astype(BF16)

    mix = jnp.dot(cat_ref[...], wo_ref[...], preferred_element_type=F32)
    z = alpha * x + (1.0 + mod_ref[2:3, :]) * mix
    o_ref[...] = _layer_norm_rows(z, ln_ref[0:1, :], ln_ref[1:2, :])


def _mix(x, mod, w_in, sinks, conv_w, conv_p, w_o, ln, w_cast, w_cast2, alpha):
    bsz, seq, d = x.shape
    cw = conv_w.shape[1]
    aw = d - cw
    ts = min(SEQ_TILE, seq)
    n_seq = seq // ts
    assert seq % ts == 0 and ts % WINDOW == 0 and aw % LANES == 0
    assert (aw // HEAD_DIM) // 4 == 2, "kernel packs exactly two KV heads into one lane group"
    assert w_in.shape[1] == aw + 2 * LANES + 2 * cw
    assert w_cast.shape[0] % (bsz * n_seq) == 0 and w_cast2.shape[0] == w_cast.shape[0]
    cast_map = lambda b, s: (b * n_seq + s, 0, 0)
    cast_spec = pl.BlockSpec((w_cast.shape[0] // (bsz * n_seq),) + w_cast.shape[1:], cast_map)
    cast2_spec = pl.BlockSpec((w_cast2.shape[0] // (bsz * n_seq),) + w_cast2.shape[1:], cast_map)
    kern = functools.partial(_mix_kernel, ts=ts, aw=aw, alpha=alpha)
    const = lambda b, s: (0, 0)
    resident = pl.Buffered(1)
    return pl.pallas_call(
        kern,
        grid=(bsz, seq // ts),
        in_specs=[
            pl.BlockSpec(memory_space=pltpu.SMEM),
            pl.BlockSpec((None, ts, d), lambda b, s: (b, s, 0)),
            pl.BlockSpec((None, 6, d), lambda b, s: (b, 0, 0)),
            pl.BlockSpec(w_in.shape, const, pipeline_mode=resident),
            pl.BlockSpec(conv_w.shape, const),
            pl.BlockSpec(conv_p.shape, const),
            pl.BlockSpec(w_o.shape, const, pipeline_mode=resident),
            pl.BlockSpec(ln.shape, const),
            cast_spec,
            cast2_spec,
        ],
        out_specs=[pl.BlockSpec((None, ts, d), lambda b, s: (b, s, 0)), cast_spec, cast2_spec],
        out_shape=[jax.ShapeDtypeStruct((bsz, seq, d), F32), jax.ShapeDtypeStruct(w_cast.shape, BF16),
                   jax.ShapeDtypeStruct(w_cast2.shape, BF16)],
        scratch_shapes=[
            pltpu.VMEM((ts, aw), BF16),
            pltpu.VMEM((4, ts + WINDOW, LANES), BF16),
            pltpu.VMEM((4, ts + WINDOW, LANES), BF16),
            pltpu.VMEM((ts + CONV_HIST, cw), F32),
            pltpu.VMEM((SUBLANES - 1, ts + CONV_HIST - SUBLANES, cw), F32),
            pltpu.VMEM((ts, d), BF16),
        ],
        compiler_params=pltpu.CompilerParams(
            dimension_semantics=("arbitrary", "arbitrary"), vmem_limit_bytes=VMEM_LIMIT),
        name="mix",
    )(sinks, x, mod, w_in, conv_w, conv_p, w_o, ln, w_cast, w_cast2)


def _ffn_pre_kernel(x_ref, mod_ref, wgs_ref, wus_ref, wds_ref, wrt_ref, rb_ref, wca_ref,
                    h_ref, sh_ref, ei_ref, ew_ref, cnt_ref, wca_out_ref):
    wca_out_ref[...] = wca_ref[...].astype(BF16)
    tm = x_ref.shape[0]
    n_exp = wrt_ref.shape[0]
    per = n_exp // N_EXPERT_GROUPS
    hf = x_ref[...] * (1.0 + mod_ref[4:5, :]) + mod_ref[3:4, :]
    _to_slab(h_ref, hf)
    h = hf.astype(BF16)

    gate = jnp.dot(h, wgs_ref[...], preferred_element_type=F32)
    up = jnp.dot(h, wus_ref[...], preferred_element_type=F32)
    act = (gate * jax.nn.sigmoid(gate) * up).astype(BF16)
    sh_ref[...] = jnp.dot(act, wds_ref[...], preferred_element_type=F32)

    logits = lax.dot_general(wrt_ref[...], h, (((1,), (1,)), ((), ())), preferred_element_type=F32)
    scores = jax.nn.sigmoid(logits)
    sel = scores + rb_ref[...]

    iota_p = lax.broadcasted_iota(jnp.int32, (per, tm), 0).astype(F32)
    gs_rows = []
    for g in range(N_EXPERT_GROUPS):
        blk = sel[g * per:(g + 1) * per, :]
        m1 = jnp.max(blk, axis=0, keepdims=True)
        i1 = jnp.min(jnp.where(blk == m1, iota_p, float(per)), axis=0, keepdims=True)
        m2 = jnp.max(jnp.where(iota_p == i1, NEG_INF, blk), axis=0, keepdims=True)
        gs_rows.append(m1 + m2)
    gs = jnp.concatenate(gs_rows, axis=0)
    iota_g = lax.broadcasted_iota(jnp.int32, (N_EXPERT_GROUPS, tm), 0).astype(F32)
    gmask = jnp.zeros((N_EXPERT_GROUPS, tm), jnp.bool_)
    for _ in range(TOPK_EXPERT_GROUPS):
        m = jnp.max(gs, axis=0, keepdims=True)
        gi = jnp.min(jnp.where(gs == m, iota_g, float(N_EXPERT_GROUPS)), axis=0, keepdims=True)
        hit = iota_g == gi
        gmask = gmask | hit
        gs = jnp.where(hit, NEG_INF, gs)
    emask = jnp.concatenate(
        [jnp.broadcast_to(gmask[g:g + 1, :], (per, tm)) for g in range(N_EXPERT_GROUPS)], axis=0)
    cand = jnp.where(emask, sel, NEG_INF)

    iota_e = lax.broadcasted_iota(jnp.int32, (n_exp, tm), 0).astype(F32)
    idx_rows, w_rows = [], []
    for _ in range(TOP_K):
        m = jnp.max(cand, axis=0, keepdims=True)
        ei = jnp.min(jnp.where(cand == m, iota_e, float(n_exp)), axis=0, keepdims=True)
        hit = iota_e == ei
        w_rows.append(jnp.sum(jnp.where(hit, scores, 0.0), axis=0, keepdims=True))
        idx_rows.append(ei)
        cand = jnp.where(hit, NEG_INF, cand)
    wk = jnp.concatenate(w_rows, axis=0)
    ew_ref[...] = wk / jnp.sum(wk, axis=0, keepdims=True) * ROUTED_SCALE
    ei_ref[...] = jnp.concatenate(idx_rows, axis=0).astype(jnp.int32)
    chosen = jnp.where(emask & (cand == NEG_INF), 1.0, 0.0).astype(BF16)
    cnt_ref[...] = jnp.dot(chosen, jnp.ones((tm, LANES), BF16), preferred_element_type=F32)


def _ffn_pre(x1, mod, wgs, wus, wds, wrt, rbias, w_cast, seq):
    n_tok, d = x1.shape
    tm = min(TOK_TILE, seq)
    n_steps = n_tok // tm
    assert seq % tm == 0 and n_tok % tm == 0 and d == SUBLANES * LANES
    assert w_cast.shape[0] % n_steps == 0
    cast_blk = (w_cast.shape[0] // n_steps,) + w_cast.shape[1:]
    cast_spec = pl.BlockSpec(cast_blk, lambda i: (i, 0, 0))
    n_exp = wrt.shape[0]
    const = lambda i: (0, 0)
    return pl.pallas_call(
        _ffn_pre_kernel,
        grid=(n_tok // tm,),
        in_specs=[
            pl.BlockSpec((tm, d), lambda i: (i, 0)),
            pl.BlockSpec((None, 6, d), lambda i: ((i * tm) // seq, 0, 0)),
            pl.BlockSpec(wgs.shape, const),
            pl.BlockSpec(wus.shape, const),
            pl.BlockSpec(wds.shape, const),
            pl.BlockSpec(wrt.shape, const),
            pl.BlockSpec(rbias.shape, const),
            cast_spec,
        ],
        out_specs=[
            pl.BlockSpec((tm * SUBLANES, LANES), lambda i: (i, 0)),
            pl.BlockSpec((tm, d), lambda i: (i, 0)),
            pl.BlockSpec((TOP_K, tm), lambda i: (0, i)),
            pl.BlockSpec((TOP_K, tm), lambda i: (0, i)),
            pl.BlockSpec((None, n_exp, LANES), lambda i: (i, 0, 0)),
            cast_spec,
        ],
        out_shape=[
            jax.ShapeDtypeStruct((n_tok * SUBLANES, LANES), F32),
            jax.ShapeDtypeStruct((n_tok, d), F32),
            jax.ShapeDtypeStruct((TOP_K, n_tok), jnp.int32),
            jax.ShapeDtypeStruct((TOP_K, n_tok), F32),
            jax.ShapeDtypeStruct((n_tok // tm, n_exp, LANES), F32),
            jax.ShapeDtypeStruct(w_cast.shape, BF16),
        ],
        compiler_params=pltpu.CompilerParams(
            dimension_semantics=("arbitrary",), vmem_limit_bytes=VMEM_LIMIT),
        name="ffn_pre",
    )(x1, mod, wgs, wus, wds, wrt, rbias, w_cast)


def _moe_kernel(off_ref, cnt_ref, tok_ref, w_ref, h_hbm, wg_ref, wu_ref, wd_ref, out_hbm,
                h_vmem, acc_ref, *rest, tt):
    j = pl.program_id(0)
    i = pl.program_id(1)
    n_grp = pl.num_programs(1)
    n_exp = n_grp * MOE_EXPERTS_PER_STEP
    eb = MOE_EXPERTS_PER_STEP
    xs, xd_ref, ys, yd_ref, sem = rest[:eb], rest[eb], rest[eb + 1:2 * eb + 1], rest[2 * eb + 1], rest[2 * eb + 2]
    rows = xd_ref.shape[0] // SUBLANES
    seg = MOE_SEG
    last = pl.num_programs(0) * n_exp - 1
    row_iota = lax.broadcasted_iota(jnp.int32, (rows, 1), 0)

    def slab(r):
        if isinstance(r, int):
            return pl.ds(r * SUBLANES, SUBLANES)
        return pl.ds(pl.multiple_of(r * SUBLANES, SUBLANES), SUBLANES)

    def gather_rows(x_ref, base, row_ids):
        for r in row_ids:
            x_ref[slab(r), :] = h_vmem[pl.ds(pl.multiple_of(tok_ref[0, 0, base + r], SUBLANES), SUBLANES), :]

    def scatter_rows(y_ref, base, row_ids):
        for g in range(0, len(row_ids), RMW_UNROLL):
            updates = []
            for r in row_ids[g:g + RMW_UNROLL]:
                dst = pl.ds(pl.multiple_of(tok_ref[0, 0, base + r], SUBLANES), SUBLANES)
                updates.append((dst, acc_ref[dst, :] + w_ref[0, 0, base + r] * y_ref[slab(r), :]))
            for dst, val in reversed(updates):
                acc_ref[dst, :] = val

    def expert_mlp(x_ref, y_ref, k, n_valid):
        x = _from_slab(x_ref, rows).astype(BF16)
        gate = jnp.dot(x, wg_ref[k], preferred_element_type=F32)
        up = jnp.dot(x, wu_ref[k], preferred_element_type=F32)
        act = (gate * jax.nn.sigmoid(gate) * up).astype(BF16)
        y = jnp.dot(act, wd_ref[k], preferred_element_type=F32)
        _to_slab(y_ref, jnp.where(row_iota < jnp.minimum(n_valid, seg), y, 0.0))

    def gather_loop(x_ref, base):
        def body(g, carry):
            gather_rows(x_ref, base, [g * SUBLANES + u for u in range(SUBLANES)])
            return carry
        lax.fori_loop(0, seg // SUBLANES, body, 0)

    def scatter_loop(y_ref, base, n_rows):
        def body(g, carry):
            scatter_rows(y_ref, base, [g * RMW_UNROLL + u for u in range(RMW_UNROLL)])
            return carry
        lax.fori_loop(0, (jnp.minimum(n_rows, seg) + RMW_UNROLL - 1) // RMW_UNROLL, body, 0)

    idx0 = j * n_exp + i * MOE_EXPERTS_PER_STEP

    n_super = pl.num_programs(0)
    slab_rows = tt * SUBLANES

    def h_copy(jj):
        return pltpu.make_async_copy(h_hbm.at[pl.ds(jj * slab_rows, slab_rows)], h_vmem, sem.at[0])

    def out_copy(jj):
        return pltpu.make_async_copy(acc_ref, out_hbm.at[pl.ds(jj * slab_rows, slab_rows)], sem.at[1])

    @pl.when(i == 0)
    def _():
        @pl.when(j == 0)
        def _():
            h_copy(j).start()

        h_copy(j).wait()
        for ref in (*xs, xd_ref, *ys):
            ref[...] = jnp.zeros_like(ref)
        for k in range(eb):
            gather_loop(xs[k], off_ref[idx0 + k])

        @pl.when(j > 0)
        def _():
            out_copy(j - 1).wait()

        acc_ref[...] = jnp.zeros_like(acc_ref)

    static_rows = list(range(seg))
    for k in range(eb):
        scatter_rows(ys[k], off_ref[jnp.maximum(idx0 + k - eb, 0)], static_rows)
    for k in range(eb):
        expert_mlp(xs[k], ys[k], k, cnt_ref[idx0 + k])
    for k in range(eb):
        gather_rows(xs[k], off_ref[jnp.minimum(idx0 + k + eb, last)], static_rows)

    for k in range(MOE_EXPERTS_PER_STEP):
        off = off_ref[idx0 + k]
        cnt = cnt_ref[idx0 + k]

        def extra(c, carry):
            gather_loop(xd_ref, off + c * seg)
            expert_mlp(xd_ref, yd_ref, k, cnt - c * seg)
            scatter_loop(yd_ref, off + c * seg, cnt - c * seg)
            return carry

        lax.fori_loop(1, (cnt + seg - 1) // seg, extra, 0)

    @pl.when(i == n_grp - 1)
    def _():
        @pl.when(j + 1 < n_super)
        def _():
            h_copy(j + 1).start()

        for k in range(eb):
            scatter_loop(ys[k], off_ref[idx0 + k], cnt_ref[idx0 + k])
        out_copy(j).start()

        @pl.when(j + 1 == n_super)
        def _():
            out_copy(j).wait()


def _moe(h_slab, tok_s, w_s, off, cnt, w_gate, w_up, w_down, tt):
    n_tok = h_slab.shape[0] // SUBLANES
    n_exp, d, f = w_gate.shape
    n_super = n_tok // tt
    rows = MOE_ROWS
    eb = MOE_EXPERTS_PER_STEP
    assert rows % (2 * SUBLANES) == 0 and n_exp % eb == 0
    assert MOE_SEG <= rows and MOE_SEG % SUBLANES == 0 and MOE_SEG % RMW_UNROLL == 0
    assert tok_s.shape[2] >= tt * TOP_K + rows
    w_map = lambda j, i, off, cnt: (i, 0, 0)
    lst_map = lambda j, i, off, cnt: (j, 0, 0)
    return pl.pallas_call(
        functools.partial(_moe_kernel, tt=tt),
        grid_spec=pltpu.PrefetchScalarGridSpec(
            num_scalar_prefetch=2,
            grid=(n_super, n_exp // eb),
            in_specs=[
                pl.BlockSpec((1, 1, tok_s.shape[2]), lst_map, memory_space=pltpu.SMEM),
                pl.BlockSpec((1, 1, w_s.shape[2]), lst_map, memory_space=pltpu.SMEM),
                pl.BlockSpec(memory_space=pl.ANY),
                pl.BlockSpec((eb, d, f), w_map),
                pl.BlockSpec((eb, d, f), w_map),
                pl.BlockSpec((eb, f, d), w_map),
            ],
            out_specs=pl.BlockSpec(memory_space=pl.ANY),
            scratch_shapes=[
                pltpu.VMEM((tt * SUBLANES, LANES), F32),
                pltpu.VMEM((tt * SUBLANES, LANES), F32),
                *[pltpu.VMEM((rows * SUBLANES, LANES), F32) for _ in range(2 * eb + 2)],
                pltpu.SemaphoreType.DMA((2,)),
            ],
        ),
        out_shape=jax.ShapeDtypeStruct((n_tok * SUBLANES, LANES), F32),
        compiler_params=pltpu.CompilerParams(
            dimension_semantics=("arbitrary", "arbitrary"), vmem_limit_bytes=VMEM_LIMIT),
        name="moe",
    )(off, cnt, tok_s, w_s, h_slab, w_gate, w_up, w_down)


def _dispatch_lists(eidx_t, ew_t, counts, tt):
    n_tok = eidx_t.shape[1]
    n_tiles, n_exp = counts.shape[0], counts.shape[1]
    n_super = n_tok // tt
    key = eidx_t * tt + jnp.arange(n_tok, dtype=jnp.int32) % tt

    def by_super_tile(a):
        return a.reshape(TOP_K, n_super, tt).transpose(1, 0, 2).reshape(n_super, TOP_K * tt)

    key_s, w_s = lax.sort((by_super_tile(key), by_super_tile(ew_t)), dimension=1, num_keys=1)
    tok_s = (key_s % tt) * SUBLANES
    pad = ((0, 0), (0, 0), (0, MOE_LIST_PAD))
    tok_s = jnp.pad(tok_s.reshape(n_super, 1, tt * TOP_K), pad)
    w_s = jnp.pad(w_s.reshape(n_super, 1, tt * TOP_K), pad)
    cnt = counts[:, :, 0].astype(jnp.int32).reshape(n_super, n_tiles // n_super, n_exp).sum(axis=1)
    off = jnp.cumsum(cnt, axis=1) - cnt
    return tok_s, w_s, off.reshape(-1), cnt.reshape(-1)


def _final_kernel(x_ref, sh_ref, ffn_ref, mod_ref, ln_ref, o_ref, *, alpha):
    tm = x_ref.shape[0]
    ffn = sh_ref[...] + _from_slab(ffn_ref, tm)
    z = alpha * x_ref[...] + (1.0 + mod_ref[5:6, :]) * ffn
    o_ref[...] = _layer_norm_rows(z, ln_ref[0:1, :], ln_ref[1:2, :])


def _final(x1, shared, routed_slab, mod, ln, seq, alpha):
    n_tok, d = x1.shape
    tm = min(FIN_TILE, seq)
    assert seq % tm == 0
    return pl.pallas_call(
        functools.partial(_final_kernel, alpha=alpha),
        grid=(n_tok // tm,),
        in_specs=[
            pl.BlockSpec((tm, d), lambda i: (i, 0)),
            pl.BlockSpec((tm, d), lambda i: (i, 0)),
            pl.BlockSpec((tm * SUBLANES, LANES), lambda i: (i, 0)),
            pl.BlockSpec((None, 6, d), lambda i: ((i * tm) // seq, 0, 0)),
            pl.BlockSpec(ln.shape, lambda i: (0, 0)),
        ],
        out_specs=pl.BlockSpec((tm, d), lambda i: (i, 0)),
        out_shape=jax.ShapeDtypeStruct((n_tok, d), F32),
        compiler_params=pltpu.CompilerParams(
            dimension_semantics=("arbitrary",), vmem_limit_bytes=VMEM_LIMIT),
        name="final",
    )(x1, shared, routed_slab, mod, ln)


def kernel(x, c, w_ada, b_ada, w_in, sinks, conv_w, conv_b, conv_ln_g, conv_ln_b, w_o, ln1_g, ln1_b,
           w_router, router_bias, w_gate_e, w_up_e, w_down_e, w_gate_s, w_up_s, w_down_s, ln2_g, ln2_b):
    bsz, seq, d = x.shape
    depth = w_ada.shape[0]
    n_exp = w_router.shape[2]
    n_tok = bsz * seq
    alpha = (2.0 * depth) ** 0.25
    tt = min(SUPER_TILE, n_tok)
    assert n_tok % tt == 0 and tt % min(TOK_TILE, seq) == 0

    for l in range(depth):
        mod = _ada(c, w_ada[l], b_ada[l]).reshape(bsz, 6, d)
        conv_p = jnp.stack([conv_b[l], conv_ln_g[l], conv_ln_b[l]])
        x1, wd_bf, wu_bf = _mix(x, mod, w_in[l].astype(BF16), sinks[l], conv_w[l], conv_p,
                                w_o[l].astype(BF16), jnp.stack([ln1_g[l], ln1_b[l]]),
                                w_down_e[l], w_up_e[l], alpha)
        x1 = x1.reshape(n_tok, d)
        h_slab, shared, eidx_t, ew_t, counts, wg_bf = _ffn_pre(
            x1, mod, w_gate_s[l].astype(BF16), w_up_s[l].astype(BF16), w_down_s[l].astype(BF16),
            w_router[l].T.astype(BF16), router_bias[l].reshape(n_exp, 1), w_gate_e[l], seq)
        tok_s, w_s, off, cnt = _dispatch_lists(eidx_t, ew_t, counts, tt)
        routed = _moe(h_slab, tok_s, w_s, off, cnt, wg_bf, wu_bf, wd_bf, tt)
        x = _final(x1, shared, routed, mod, jnp.stack([ln2_g[l], ln2_b[l]]), seq, alpha)
        x = x.reshape(bsz, seq, d)
    return x
```

```python
import functools
import math

import jax
import jax.numpy as jnp
from jax import lax
from jax.experimental import pallas as pl
from jax.experimental.pallas import tpu as pltpu

F32 = jnp.float32
BF16 = jnp.bfloat16
NEG_INF = float("-inf")

HEAD_DIM = 64
WINDOW = 128
CONV_KERNEL = 31
CONV_HIST = 32
TOP_K = 8
N_EXPERT_GROUPS = 8
TOPK_EXPERT_GROUPS = 4
ROUTED_SCALE = 2.5
LN_EPS = 1e-5

LANES = 128
SUBLANES = 8
SEQ_TILE = 512
TOK_TILE = 512
FIN_TILE = 512
SUPER_TILE = 4096
MOE_ROWS = 176
MOE_SEG = 160
MOE_EXPERTS_PER_STEP = 4
MOE_LIST_PAD = 256
RMW_UNROLL = 8
CONV_ROWS = 64
VMEM_LIMIT = 56 * 1024 * 1024


def _alibi_slopes(n_heads):
    return [2.0 ** (-8.0 * (i + 1) / n_heads) for i in range(n_heads)]


def _layer_norm_rows(z, g, b):
    mu = jnp.mean(z, axis=-1, keepdims=True)
    d = z - mu
    var = jnp.mean(d * d, axis=-1, keepdims=True)
    return d * lax.rsqrt(var + LN_EPS) * g + b


def _to_slab(ref, val):
    n = val.shape[0]
    for s in range(SUBLANES):
        ref[pl.ds(s, n, stride=SUBLANES), :] = val[:, s * LANES:(s + 1) * LANES]


def _from_slab(ref, n):
    return jnp.concatenate([ref[pl.ds(s, n, stride=SUBLANES), :] for s in range(SUBLANES)], axis=1)


def _ada_kernel(c_ref, w_ref, b_ref, o_ref):
    c = c_ref[...]
    ca = (c * jax.nn.sigmoid(c)).astype(BF16)
    o_ref[...] = jnp.dot(ca, w_ref[...].astype(BF16), preferred_element_type=F32) + b_ref[...]


def _ada(c, w_ada, b_ada):
    bsz, d = c.shape
    n_out = w_ada.shape[1]
    return pl.pallas_call(
        _ada_kernel,
        grid=(n_out // d,),
        in_specs=[
            pl.BlockSpec((bsz, d), lambda j: (0, 0)),
            pl.BlockSpec((d, d), lambda j: (0, j)),
            pl.BlockSpec((1, d), lambda j: (0, j)),
        ],
        out_specs=pl.BlockSpec((bsz, d), lambda j: (0, j)),
        out_shape=jax.ShapeDtypeStruct((bsz, n_out), F32),
        compiler_params=pltpu.CompilerParams(vmem_limit_bytes=VMEM_LIMIT),
        name="ada",
    )(c, w_ada, b_ada.reshape(1, n_out))


def _mix_kernel(sinks_ref, x_ref, mod_ref, win_ref, convw_ref, convp_ref, wo_ref, ln_ref, wca_ref, wcb_ref,
                o_ref, wca_out_ref, wcb_out_ref, q_ref, ke_ref, ve_ref, glu_ref, gsh_ref, cat_ref,
                *, ts, aw, alpha):
    wca_out_ref[...] = wca_ref[...].astype(BF16)
    wcb_out_ref[...] = wcb_ref[...].astype(BF16)
    s_idx = pl.program_id(1)
    n_heads = aw // HEAD_DIM
    slopes = _alibi_slopes(n_heads)
    cw = cat_ref.shape[1] - aw

    @pl.when(s_idx == 0)
    def _():
        ke_ref[:, 0:WINDOW, :] = jnp.zeros((4, WINDOW, LANES), BF16)
        ve_ref[:, 0:WINDOW, :] = jnp.zeros((4, WINDOW, LANES), BF16)
        glu_ref[0:CONV_HIST, :] = jnp.zeros((CONV_HIST, cw), F32)

    @pl.when(s_idx > 0)
    def _():
        ke_ref[:, 0:WINDOW, :] = ke_ref[:, ts:ts + WINDOW, :]
        ve_ref[:, 0:WINDOW, :] = ve_ref[:, ts:ts + WINDOW, :]
        glu_ref[0:CONV_HIST, :] = glu_ref[ts:ts + CONV_HIST, :]

    x = x_ref[...]
    h = (x * (1.0 + mod_ref[1:2, :]) + mod_ref[0:1, :]).astype(BF16)

    q = jnp.dot(h, win_ref[:, 0:aw], preferred_element_type=F32)
    q_ref[...] = (q * (1.0 / math.sqrt(HEAD_DIM))).astype(BF16)
    kv = jnp.dot(h, win_ref[:, aw:aw + 2 * LANES], preferred_element_type=F32)
    lo = lax.broadcasted_iota(jnp.int32, (ts, LANES), 1) < HEAD_DIM
    for dst, t in ((ke_ref, kv[:, 0:LANES]), (ve_ref, kv[:, LANES:2 * LANES])):
        t_r = pltpu.roll(t, HEAD_DIM, axis=1)
        dst[0, WINDOW:WINDOW + ts, :] = jnp.where(lo, t, 0.0).astype(BF16)
        dst[1, WINDOW:WINDOW + ts, :] = jnp.where(lo, 0.0, t_r).astype(BF16)
        dst[2, WINDOW:WINDOW + ts, :] = jnp.where(lo, t_r, 0.0).astype(BF16)
        dst[3, WINDOW:WINDOW + ts, :] = jnp.where(lo, 0.0, t).astype(BF16)
    u0 = aw + 2 * LANES
    ga = jnp.dot(h, win_ref[:, u0:u0 + cw], preferred_element_type=F32)
    gb = jnp.dot(h, win_ref[:, u0 + cw:u0 + 2 * cw], preferred_element_type=F32)
    glu_ref[CONV_HIST:CONV_HIST + ts, :] = ga * jax.nn.sigmoid(gb)

    qi = lax.broadcasted_iota(jnp.int32, (WINDOW, 2 * WINDOW), 0)
    kj = lax.broadcasted_iota(jnp.int32, (WINDOW, 2 * WINDOW), 1)
    dist = WINDOW + qi - kj
    band = (dist >= 0) & (dist < WINDOW)
    neg_dist = -dist.astype(F32)
    bias_any = jnp.where(band, neg_dist, NEG_INF)
    bias_first = jnp.where(band & ((kj >= WINDOW) | (s_idx > 0)), neg_dist, NEG_INF)
    for i in range(ts // WINDOW):
        r0 = i * WINDOW
        bias = bias_first if i == 0 else bias_any
        for pair in range(aw // LANES):
            g = (2 * pair) // (n_heads // 2)
            qp = q_ref[r0:r0 + WINDOW, pair * LANES:(pair + 1) * LANES]
            out_pair = None
            for par in range(2):
                hd = 2 * pair + par
                kk = ke_ref[2 * g + par, r0:r0 + 2 * WINDOW, :]
                s = lax.dot_general(qp, kk, (((1,), (1,)), ((), ())), preferred_element_type=F32)
                s = s + slopes[hd] * bias
                sink = sinks_ref[hd]
                m = jnp.maximum(jnp.max(s, axis=-1, keepdims=True), sink)
                p = jnp.exp(s - m)
                denom = jnp.sum(p, axis=-1, keepdims=True) + jnp.exp(sink - m)
                vv = ve_ref[2 * g + par, r0:r0 + 2 * WINDOW, :]
                o = jnp.dot(p.astype(BF16), vv, preferred_element_type=F32) * (1.0 / denom)
                out_pair = o if out_pair is None else out_pair + o
            cat_ref[r0:r0 + WINDOW, pair * LANES:(pair + 1) * LANES] = out_pair.astype(BF16)

    conv_b = convp_ref[0:1, :]
    cln_g = convp_ref[1:2, :]
    cln_b = convp_ref[2:3, :]
    off = CONV_HIST - (CONV_KERNEL - 1)
    n_sh = gsh_ref.shape[1]
    for p in range(1, SUBLANES):
        gsh_ref[p - 1] = glu_ref[p:p + n_sh, :]
    for c in range(ts // CONV_ROWS):
        c0 = c * CONV_ROWS
        acc = jnp.broadcast_to(conv_b, (CONV_ROWS, cw))
        for j in range(CONV_KERNEL):
            a, p = divmod(off + j, SUBLANES)
            r0 = c0 + a * SUBLANES
            tap = glu_ref[r0:r0 + CONV_ROWS, :] if p == 0 else gsh_ref[p - 1, r0:r0 + CONV_ROWS, :]
            acc = acc + tap * convw_ref[j:j + 1, :]
        yn = _layer_norm_rows(acc, cln_g, cln_b)
        cat_ref[c0:c0 + CONV_ROWS, aw:aw + cw] = (yn * jax.nn.sigmoid(yn)).astype(BF16)

    mix = jnp.dot(cat_ref[...], wo_ref[...], preferred_element_type=F32)
    z = alpha * x + (1.0 + mod_ref[2:3, :]) * mix
    o_ref[...] = _layer_norm_rows(z, ln_ref[0:1, :], ln_ref[1:2, :])


def _mix(x, mod, w_in, sinks, conv_w, conv_p, w_o, ln, w_cast, w_cast2, alpha):
    bsz, seq, d = x.shape
    cw = conv_w.shape[1]
    aw = d - cw
    ts = min(SEQ_TILE, seq)
    n_seq = seq // ts
    assert seq % ts == 0 and ts % WINDOW == 0 and aw % LANES == 0
    assert (aw // HEAD_DIM) // 4 == 2, "kernel packs exactly two KV heads into one lane group"
    assert w_in.shape[1] == aw + 2 * LANES + 2 * cw
    assert w_cast.shape[0] % (bsz * n_seq) == 0 and w_cast2.shape[0] == w_cast.shape[0]
    cast_map = lambda b, s: (b * n_seq + s, 0, 0)
    cast_spec = pl.BlockSpec((w_cast.shape[0] // (bsz * n_seq),) + w_cast.shape[1:], cast_map)
    cast2_spec = pl.BlockSpec((w_cast2.shape[0] // (bsz * n_seq),) + w_cast2.shape[1:], cast_map)
    kern = functools.partial(_mix_kernel, ts=ts, aw=aw, alpha=alpha)
    const = lambda b, s: (0, 0)
    resident = pl.Buffered(1)
    return pl.pallas_call(
        kern,
        grid=(bsz, seq // ts),
        in_specs=[
            pl.BlockSpec(memory_space=pltpu.SMEM),
            pl.BlockSpec((None, ts, d), lambda b, s: (b, s, 0)),
            pl.BlockSpec((None, 6, d), lambda b, s: (b, 0, 0)),
            pl.BlockSpec(w_in.shape, const, pipeline_mode=resident),
            pl.BlockSpec(conv_w.shape, const),
            pl.BlockSpec(conv_p.shape, const),
            pl.BlockSpec(w_o.shape, const, pipeline_mode=resident),
            pl.BlockSpec(ln.shape, const),
            cast_spec,
            cast2_spec,
        ],
        out_specs=[pl.BlockSpec((None, ts, d), lambda b, s: (b, s, 0)), cast_spec, cast2_spec],
        out_shape=[jax.ShapeDtypeStruct((bsz, seq, d), F32), jax.ShapeDtypeStruct(w_cast.shape, BF16),
                   jax.ShapeDtypeStruct(w_cast2.shape, BF16)],
        scratch_shapes=[
            pltpu.VMEM((ts, aw), BF16),
            pltpu.VMEM((4, ts + WINDOW, LANES), BF16),
            pltpu.VMEM((4, ts + WINDOW, LANES), BF16),
            pltpu.VMEM((ts + CONV_HIST, cw), F32),
            pltpu.VMEM((SUBLANES - 1, ts + CONV_HIST - SUBLANES, cw), F32),
            pltpu.VMEM((ts, d), BF16),
        ],
        compiler_params=pltpu.CompilerParams(
            dimension_semantics=("arbitrary", "arbitrary"), vmem_limit_bytes=VMEM_LIMIT),
        name="mix",
    )(sinks, x, mod, w_in, conv_w, conv_p, w_o, ln, w_cast, w_cast2)


def _ffn_pre_kernel(x_ref, mod_ref, wgs_ref, wus_ref, wds_ref, wrt_ref, rb_ref, wca_ref,
                    h_ref, sh_ref, ei_ref, ew_ref, cnt_ref, wca_out_ref):
    wca_out_ref[...] = wca_ref[...].astype(BF16)
    tm = x_ref.shape[0]
    n_exp = wrt_ref.shape[0]
    per = n_exp // N_EXPERT_GROUPS
    hf = x_ref[...] * (1.0 + mod_ref[4:5, :]) + mod_ref[3:4, :]
    _to_slab(h_ref, hf)
    h = hf.astype(BF16)

    gate = jnp.dot(h, wgs_ref[...], preferred_element_type=F32)
    up = jnp.dot(h, wus_ref[...], preferred_element_type=F32)
    act = (gate * jax.nn.sigmoid(gate) * up).astype(BF16)
    sh_ref[...] = jnp.dot(act, wds_ref[...], preferred_element_type=F32)

    logits = lax.dot_general(wrt_ref[...], h, (((1,), (1,)), ((), ())), preferred_element_type=F32)
    scores = jax.nn.sigmoid(logits)
    sel = scores + rb_ref[...]

    iota_p = lax.broadcasted_iota(jnp.int32, (per, tm), 0).astype(F32)
    gs_rows = []
    for g in range(N_EXPERT_GROUPS):
        blk = sel[g * per:(g + 1) * per, :]
        m1 = jnp.max(blk, axis=0, keepdims=True)
        i1 = jnp.min(jnp.where(blk == m1, iota_p, float(per)), axis=0, keepdims=True)
        m2 = jnp.max(jnp.where(iota_p == i1, NEG_INF, blk), axis=0, keepdims=True)
        gs_rows.append(m1 + m2)
    gs = jnp.concatenate(gs_rows, axis=0)
    iota_g = lax.broadcasted_iota(jnp.int32, (N_EXPERT_GROUPS, tm), 0).astype(F32)
    gmask = jnp.zeros((N_EXPERT_GROUPS, tm), jnp.bool_)
    for _ in range(TOPK_EXPERT_GROUPS):
        m = jnp.max(gs, axis=0, keepdims=True)
        gi = jnp.min(jnp.where(gs == m, iota_g, float(N_EXPERT_GROUPS)), axis=0, keepdims=True)
        hit = iota_g == gi
        gmask = gmask | hit
        gs = jnp.where(hit, NEG_INF, gs)
    emask = jnp.concatenate(
        [jnp.broadcast_to(gmask[g:g + 1, :], (per, tm)) for g in range(N_EXPERT_GROUPS)], axis=0)
    cand = jnp.where(emask, sel, NEG_INF)

    iota_e = lax.broadcasted_iota(jnp.int32, (n_exp, tm), 0).astype(F32)
    idx_rows, w_rows = [], []
    for _ in range(TOP_K):
        m = jnp.max(cand, axis=0, keepdims=True)
        ei = jnp.min(jnp.where(cand == m, iota_e, float(n_exp)), axis=0, keepdims=True)
        hit = iota_e == ei
        w_rows.append(jnp.sum(jnp.where(hit, scores, 0.0), axis=0, keepdims=True))
        idx_rows.append(ei)
        cand = jnp.where(hit, NEG_INF, cand)
    wk = jnp.concatenate(w_rows, axis=0)
    ew_ref[...] = wk / jnp.sum(wk, axis=0, keepdims=True) * ROUTED_SCALE
    ei_ref[...] = jnp.concatenate(idx_rows, axis=0).astype(jnp.int32)
    chosen = jnp.where(emask & (cand == NEG_INF), 1.0, 0.0).astype(BF16)
    cnt_ref[...] = jnp.dot(chosen, jnp.ones((tm, LANES), BF16), preferred_element_type=F32)


def _ffn_pre(x1, mod, wgs, wus, wds, wrt, rbias, w_cast, seq):
    n_tok, d = x1.shape
    tm = min(TOK_TILE, seq)
    n_steps = n_tok // tm
    assert seq % tm == 0 and n_tok % tm == 0 and d == SUBLANES * LANES
    assert w_cast.shape[0] % n_steps == 0
    cast_blk = (w_cast.shape[0] // n_steps,) + w_cast.shape[1:]
    cast_spec = pl.BlockSpec(cast_blk, lambda i: (i, 0, 0))
    n_exp = wrt.shape[0]
    const = lambda i: (0, 0)
    return pl.pallas_call(
        _ffn_pre_kernel,
        grid=(n_tok // tm,),
        in_specs=[
            pl.BlockSpec((tm, d), lambda i: (i, 0)),
            pl.BlockSpec((None, 6, d), lambda i: ((i * tm) // seq, 0, 0)),
            pl.BlockSpec(wgs.shape, const),
            pl.BlockSpec(wus.shape, const),
            pl.BlockSpec(wds.shape, const),
            pl.BlockSpec(wrt.shape, const),
            pl.BlockSpec(rbias.shape, const),
            cast_spec,
        ],
        out_specs=[
            pl.BlockSpec((tm * SUBLANES, LANES), lambda i: (i, 0)),
            pl.BlockSpec((tm, d), lambda i: (i, 0)),
            pl.BlockSpec((TOP_K, tm), lambda i: (0, i)),
            pl.BlockSpec((TOP_K, tm), lambda i: (0, i)),
            pl.BlockSpec((None, n_exp, LANES), lambda i: (i, 0, 0)),
            cast_spec,
        ],
        out_shape=[
            jax.ShapeDtypeStruct((n_tok * SUBLANES, LANES), F32),
            jax.ShapeDtypeStruct((n_tok, d), F32),
            jax.ShapeDtypeStruct((TOP_K, n_tok), jnp.int32),
            jax.ShapeDtypeStruct((TOP_K, n_tok), F32),
            jax.ShapeDtypeStruct((n_tok // tm, n_exp, LANES), F32),
            jax.ShapeDtypeStruct(w_cast.shape, BF16),
        ],
        compiler_params=pltpu.CompilerParams(
            dimension_semantics=("arbitrary",), vmem_limit_bytes=VMEM_LIMIT),
        name="ffn_pre",
    )(x1, mod, wgs, wus, wds, wrt, rbias, w_cast)


def _moe_kernel(off_ref, cnt_ref, tok_ref, w_ref, h_hbm, wg_ref, wu_ref, wd_ref, out_hbm,
                h_vmem, acc_ref, *rest, tt):
    j = pl.program_id(0)
    i = pl.program_id(1)
    n_grp = pl.num_programs(1)
    n_exp = n_grp * MOE_EXPERTS_PER_STEP
    eb = MOE_EXPERTS_PER_STEP
    xs, xd_ref, ys, yd_ref, sem = rest[:eb], rest[eb], rest[eb + 1:2 * eb + 1], rest[2 * eb + 1], rest[2 * eb + 2]
    rows = xd_ref.shape[0] // SUBLANES
    seg = MOE_SEG
    last = pl.num_programs(0) * n_exp - 1
    row_iota = lax.broadcasted_iota(jnp.int32, (rows, 1), 0)

    def slab(r):
        if isinstance(r, int):
            return pl.ds(r * SUBLANES, SUBLANES)
        return pl.ds(pl.multiple_of(r * SUBLANES, SUBLANES), SUBLANES)

    def gather_rows(x_ref, base, row_ids):
        for r in row_ids:
            x_ref[slab(r), :] = h_vmem[pl.ds(pl.multiple_of(tok_ref[0, 0, base + r], SUBLANES), SUBLANES), :]

    def scatter_rows(y_ref, base, row_ids):
        for g in range(0, len(row_ids), RMW_UNROLL):
            updates = []
            for r in row_ids[g:g + RMW_UNROLL]:
                dst = pl.ds(pl.multiple_of(tok_ref[0, 0, base + r], SUBLANES), SUBLANES)
                updates.append((dst, acc_ref[dst, :] + w_ref[0, 0, base + r] * y_ref[slab(r), :]))
            for dst, val in reversed(updates):
                acc_ref[dst, :] = val

    def expert_mlp(x_ref, y_ref, k, n_valid):
        x = _from_slab(x_ref, rows).astype(BF16)
        gate = jnp.dot(x, wg_ref[k], preferred_element_type=F32)
        up = jnp.dot(x, wu_ref[k], preferred_element_type=F32)
        act = (gate * jax.nn.sigmoid(gate) * up).astype(BF16)
        y = jnp.dot(act, wd_ref[k], preferred_element_type=F32)
        _to_slab(y_ref, jnp.where(row_iota < jnp.minimum(n_valid, seg), y, 0.0))

    def gather_loop(x_ref, base):
        def body(g, carry):
            gather_rows(x_ref, base, [g * SUBLANES + u for u in range(SUBLANES)])
            return carry
        lax.fori_loop(0, seg // SUBLANES, body, 0)

    def scatter_loop(y_ref, base, n_rows):
        def body(g, carry):
            scatter_rows(y_ref, base, [g * RMW_UNROLL + u for u in range(RMW_UNROLL)])
            return carry
        lax.fori_loop(0, (jnp.minimum(n_rows, seg) + RMW_UNROLL - 1) // RMW_UNROLL, body, 0)

    idx0 = j * n_exp + i * MOE_EXPERTS_PER_STEP

    n_super = pl.num_programs(0)
    slab_rows = tt * SUBLANES

    def h_copy(jj):
        return pltpu.make_async_copy(h_hbm.at[pl.ds(jj * slab_rows, slab_rows)], h_vmem, sem.at[0])

    def out_copy(jj):
        return pltpu.make_async_copy(acc_ref, out_hbm.at[pl.ds(jj * slab_rows, slab_rows)], sem.at[1])

    @pl.when(i == 0)
    def _():
        @pl.when(j == 0)
        def _():
            h_copy(j).start()

        h_copy(j).wait()
        for ref in (*xs, xd_ref, *ys):
            ref[...] = jnp.zeros_like(ref)
        for k in range(eb):
            gather_loop(xs[k], off_ref[idx0 + k])

        @pl.when(j > 0)
        def _():
            out_copy(j - 1).wait()

        acc_ref[...] = jnp.zeros_like(acc_ref)

    static_rows = list(range(seg))
    for k in range(eb):
        scatter_rows(ys[k], off_ref[jnp.maximum(idx0 + k - eb, 0)], static_rows)
    for k in range(eb):
        expert_mlp(xs[k], ys[k], k, cnt_ref[idx0 + k])
    for k in range(eb):
        gather_rows(xs[k], off_ref[jnp.minimum(idx0 + k + eb, last)], static_rows)

    for k in range(MOE_EXPERTS_PER_STEP):
        off = off_ref[idx0 + k]
        cnt = cnt_ref[idx0 + k]

        def extra(c, carry):
            gather_loop(xd_ref, off + c * seg)
            expert_mlp(xd_ref, yd_ref, k, cnt - c * seg)
            scatter_loop(yd_ref, off + c * seg, cnt - c * seg)
            return carry

        lax.fori_loop(1, (cnt + seg - 1) // seg, extra, 0)

    @pl.when(i == n_grp - 1)
    def _():
        @pl.when(j + 1 < n_super)
        def _():
            h_copy(j + 1).start()

        for k in range(eb):
            scatter_loop(ys[k], off_ref[idx0 + k], cnt_ref[idx0 + k])
        out_copy(j).start()

        @pl.when(j + 1 == n_super)
        def _():
            out_copy(j).wait()


def _moe(h_slab, tok_s, w_s, off, cnt, w_gate, w_up, w_down, tt):
    n_tok = h_slab.shape[0] // SUBLANES
    n_exp, d, f = w_gate.shape
    n_super = n_tok // tt
    rows = MOE_ROWS
    eb = MOE_EXPERTS_PER_STEP
    assert rows % (2 * SUBLANES) == 0 and n_exp % eb == 0
    assert MOE_SEG <= rows and MOE_SEG % SUBLANES == 0 and MOE_SEG % RMW_UNROLL == 0
    assert tok_s.shape[2] >= tt * TOP_K + rows
    w_map = lambda j, i, off, cnt: (i, 0, 0)
    lst_map = lambda j, i, off, cnt: (j, 0, 0)
    return pl.pallas_call(
        functools.partial(_moe_kernel, tt=tt),
        grid_spec=pltpu.PrefetchScalarGridSpec(
            num_scalar_prefetch=2,
            grid=(n_super, n_exp // eb),
            in_specs=[
                pl.BlockSpec((1, 1, tok_s.shape[2]), lst_map, memory_space=pltpu.SMEM),
                pl.BlockSpec((1, 1, w_s.shape[2]), lst_map, memory_space=pltpu.SMEM),
                pl.BlockSpec(memory_space=pl.ANY),
                pl.BlockSpec((eb, d, f), w_map),
                pl.BlockSpec((eb, d, f), w_map),
                pl.BlockSpec((eb, f, d), w_map),
            ],
            out_specs=pl.BlockSpec(memory_space=pl.ANY),
            scratch_shapes=[
                pltpu.VMEM((tt * SUBLANES, LANES), F32),
                pltpu.VMEM((tt * SUBLANES, LANES), F32),
                *[pltpu.VMEM((rows * SUBLANES, LANES), F32) for _ in range(2 * eb + 2)],
                pltpu.SemaphoreType.DMA((2,)),
            ],
        ),
        out_shape=jax.ShapeDtypeStruct((n_tok * SUBLANES, LANES), F32),
        compiler_params=pltpu.CompilerParams(
            dimension_semantics=("arbitrary", "arbitrary"), vmem_limit_bytes=VMEM_LIMIT),
        name="moe",
    )(off, cnt, tok_s, w_s, h_slab, w_gate, w_up, w_down)


def _dispatch_lists(eidx_t, ew_t, counts, tt):
    n_tok = eidx_t.shape[1]
    n_tiles, n_exp = counts.shape[0], counts.shape[1]
    n_super = n_tok // tt
    key = eidx_t * tt + jnp.arange(n_tok, dtype=jnp.int32) % tt

    def by_super_tile(a):
        return a.reshape(TOP_K, n_super, tt).transpose(1, 0, 2).reshape(n_super, TOP_K * tt)

    key_s, w_s = lax.sort((by_super_tile(key), by_super_tile(ew_t)), dimension=1, num_keys=1)
    tok_s = (key_s % tt) * SUBLANES
    pad = ((0, 0), (0, 0), (0, MOE_LIST_PAD))
    tok_s = jnp.pad(tok_s.reshape(n_super, 1, tt * TOP_K), pad)
    w_s = jnp.pad(w_s.reshape(n_super, 1, tt * TOP_K), pad)
    cnt = counts[:, :, 0].astype(jnp.int32).reshape(n_super, n_tiles // n_super, n_exp).sum(axis=1)
    off = jnp.cumsum(cnt, axis=1) - cnt
    return tok_s, w_s, off.reshape(-1), cnt.reshape(-1)


def _final_kernel(x_ref, sh_ref, ffn_ref, mod_ref, ln_ref, o_ref, *, alpha):
    tm = x_ref.shape[0]
    ffn = sh_ref[...] + _from_slab(ffn_ref, tm)
    z = alpha * x_ref[...] + (1.0 + mod_ref[5:6, :]) * ffn
    o_ref[...] = _layer_norm_rows(z, ln_ref[0:1, :], ln_ref[1:2, :])


def _final(x1, shared, routed_slab, mod, ln, seq, alpha):
    n_tok, d = x1.shape
    tm = min(FIN_TILE, seq)
    assert seq % tm == 0
    return pl.pallas_call(
        functools.partial(_final_kernel, alpha=alpha),
        grid=(n_tok // tm,),
        in_specs=[
            pl.BlockSpec((tm, d), lambda i: (i, 0)),
            pl.BlockSpec((tm, d), lambda i: (i, 0)),
            pl.BlockSpec((tm * SUBLANES, LANES), lambda i: (i, 0)),
            pl.BlockSpec((None, 6, d), lambda i: ((i * tm) // seq, 0, 0)),
            pl.BlockSpec(ln.shape, lambda i: (0, 0)),
        ],
        out_specs=pl.BlockSpec((tm, d), lambda i: (i, 0)),
        out_shape=jax.ShapeDtypeStruct((n_tok, d), F32),
        compiler_params=pltpu.CompilerParams(
            dimension_semantics=("arbitrary",), vmem_limit_bytes=VMEM_LIMIT),
        name="final",
    )(x1, shared, routed_slab, mod, ln)


def kernel(x, c, w_ada, b_ada, w_in, sinks, conv_w, conv_b, conv_ln_g, conv_ln_b, w_o, ln1_g, ln1_b,
           w_router, router_bias, w_gate_e, w_up_e, w_down_e, w_gate_s, w_up_s, w_down_s, ln2_g, ln2_b):
    bsz, seq, d = x.shape
    depth = w_ada.shape[0]
    n_exp = w_router.shape[2]
    n_tok = bsz * seq
    alpha = (2.0 * depth) ** 0.25
    tt = min(SUPER_TILE, n_tok)
    assert n_tok % tt == 0 and tt % min(TOK_TILE, seq) == 0

    for l in range(depth):
        mod = _ada(c, w_ada[l], b_ada[l]).reshape(bsz, 6, d)
        conv_p = jnp.stack([conv_b[l], conv_ln_g[l], conv_ln_b[l]])
        x1, wd_bf, wu_bf = _mix(x, mod, w_in[l].astype(BF16), sinks[l], conv_w[l], conv_p,
                                w_o[l].astype(BF16), jnp.stack([ln1_g[l], ln1_b[l]]),
                                w_down_e[l], w_up_e[l], alpha)
        x1 = x1.reshape(n_tok, d)
        h_slab, shared, eidx_t, ew_t, counts, wg_bf = _ffn_pre(
            x1, mod, w_gate_s[l].astype(BF16), w_up_s[l].astype(BF16), w_down_s[l].astype(BF16),
            w_router[l].T.astype(BF16), router_bias[l].reshape(n_exp, 1), w_gate_e[l], seq)
        tok_s, w_s, off, cnt = _dispatch_lists(eidx_t, ew_t, counts, tt)
        routed = _moe(h_slab, tok_s, w_s, off, cnt, wg_bf, wu_bf, wd_bf, tt)
        x = _final(x1, shared, routed, mod, jnp.stack([ln2_g[l], ln2_b[l]]), seq, alpha)
        x = x.reshape(bsz, seq, d)
    return x
```
